```python
import functools
import jax, jax.numpy as jnp
from jax import lax
import numpy as np

D_MODEL = 2048
BATCH = 1
SEQ = 16384
DEPTH = 1
DEC_BATCH = 32
DEC_SEQ = 1
PAST_LEN = 16384
PAGE_SIZE = 128

HEAD_DIM = 128
GDN_HEADS = 8
MOBA_HEADS = 8
GDN_WIDTH = GDN_HEADS * HEAD_DIM
MOBA_WIDTH = MOBA_HEADS * HEAD_DIM
MIX_WIDTH = GDN_WIDTH + MOBA_WIDTH
CONV_WIDTH = 4
GDN_CONV_CH = 3 * GDN_WIDTH
GDN_COLS = 4 * GDN_WIDTH + 2 * GDN_HEADS
IN_COLS = GDN_COLS + 3 * MOBA_WIDTH
GDN_CHUNK = 64
MOBA_BLOCK = 256
MOBA_TOPK = 3
MOBA_Q_BLOCK = 128
ROPE_DIM = HEAD_DIM // 4
ROPE_THETA = 500000.0
CROSS_HEADS = 4
CROSS_WIDTH = CROSS_HEADS * HEAD_DIM
MEM_TOKENS = 256
D_FF = 5632
NORM_EPS = 1e-6

kernel_name = 'hymba_gdn_moba_macaron_step'

F32 = jnp.float32


def rms_norm(x, g):
    xf = x.astype(F32)
    y = xf * lax.rsqrt(jnp.mean(xf * xf, axis=-1, keepdims=True) + NORM_EPS)
    return (y * g.astype(F32)).astype(x.dtype)


def swiglu(x, w_gu, w_down):
    gate, up = jnp.split(x @ w_gu, 2, axis=-1)
    return (jax.nn.silu(gate) * up) @ w_down


def rope_partial(x, pos):
    half = ROPE_DIM // 2
    inv_freq = ROPE_THETA ** (-jnp.arange(0, ROPE_DIM, 2, dtype=F32) / ROPE_DIM)
    ang = pos.astype(F32)[:, None] * inv_freq[None, :]
    cos = jnp.cos(ang)[None, :, None, :]
    sin = jnp.sin(ang)[None, :, None, :]
    xf = x.astype(F32)
    x1, x2 = xf[..., :half], xf[..., half:ROPE_DIM]
    out = jnp.concatenate([x1 * cos - x2 * sin, x2 * cos + x1 * sin, xf[..., ROPE_DIM:]], axis=-1)
    return out.astype(x.dtype)


def causal_short_conv(x, buf, w):
    L = x.shape[1]
    xp = jnp.concatenate([buf.astype(x.dtype), x], axis=1)
    y = xp[:, 0:L] * w[0]
    for j in range(1, CONV_WIDTH):
        y = y + xp[:, j:j + L] * w[j]
    return jax.nn.silu(y), xp[:, L:]


def l2_normalize(x):
    xf = x.astype(F32)
    return xf * lax.rsqrt(jnp.sum(xf * xf, axis=-1, keepdims=True) + NORM_EPS)


def gated_delta_chunked(q, k, v, g, beta, s0):
    B, L, H, DK = q.shape
    DV = v.shape[-1]
    C = min(GDN_CHUNK, L)
    n = -(-L // C)
    pad = n * C - L

    def chunks(x):
        x = jnp.pad(x, [(0, 0), (0, pad)] + [(0, 0)] * (x.ndim - 2))
        x = x.reshape((B, n, C) + x.shape[2:])
        return jnp.moveaxis(x, 2, 3).swapaxes(0, 1)

    qc = chunks(q * DK ** -0.5)
    kc, vc, bc = chunks(k), chunks(v), chunks(beta)
    gc = jnp.cumsum(chunks(g), axis=-1)
    causal = jnp.tril(jnp.ones((C, C), bool))
    strict = jnp.tril(jnp.ones((C, C), bool), -1)
    eye = jnp.eye(C, dtype=F32)

    def step(S, inp):
        qi, ki, vi, gi, bi = inp
        decay = jnp.exp(jnp.where(causal, gi[..., :, None] - gi[..., None, :], -jnp.inf))
        kb = ki * bi[..., None]
        a = jnp.einsum('bhid,bhjd->bhij', kb, ki) * jnp.where(strict, decay, 0.0)
        t = lax.linalg.triangular_solve(eye + a, jnp.broadcast_to(eye, a.shape),
                                        left_side=True, lower=True, unit_diagonal=True)
        u = t @ (vi * bi[..., None])
        w = t @ (kb * jnp.exp(gi)[..., None])
        v_new = u - w @ S
        attn = jnp.einsum('bhid,bhjd->bhij', qi, ki) * decay
        o = (qi * jnp.exp(gi)[..., None]) @ S + attn @ v_new
        g_last = gi[..., -1:]
        S = S * jnp.exp(g_last)[..., None] + jnp.einsum(
            'bhcd,bhce->bhde', ki * jnp.exp(g_last - gi)[..., None], v_new)
        return S, o

    s_fin, o = lax.scan(step, s0.astype(F32), (qc, kc, vc, gc, bc))
    o = jnp.moveaxis(o.swapaxes(0, 1), 3, 2).reshape(B, n * C, H, DV)[:, :L]
    return o, s_fin


def gdn_mixer(cols, conv_buf, s0, conv_w, a_log, dt_bias, out_norm):
    B, L, _ = cols.shape
    qkv, conv_new = causal_short_conv(cols[..., :GDN_CONV_CH], conv_buf, conv_w)
    z = cols[..., GDN_CONV_CH:4 * GDN_WIDTH]
    b = cols[..., 4 * GDN_WIDTH:4 * GDN_WIDTH + GDN_HEADS]
    a = cols[..., 4 * GDN_WIDTH + GDN_HEADS:]
    shp = (B, L, GDN_HEADS, HEAD_DIM)
    q, k, v = jnp.split(qkv, 3, axis=-1)
    q = l2_normalize(q.reshape(shp))
    k = l2_normalize(k.reshape(shp))
    v = v.reshape(shp).astype(F32)
    beta = jax.nn.sigmoid(b.astype(F32))
    g = -jnp.exp(a_log.astype(F32)) * jax.nn.softplus(a.astype(F32) + dt_bias.astype(F32))
    o, s_new = gated_delta_chunked(q, k, v, g, beta, s0)
    o = o * lax.rsqrt(jnp.mean(o * o, axis=-1, keepdims=True) + NORM_EPS) * out_norm.astype(F32)
    o = o * jax.nn.silu(z.reshape(shp).astype(F32))
    return o.reshape(B, L, GDN_WIDTH).astype(cols.dtype), s_new.astype(s0.dtype), conv_new


def moba_prompt(q, k, v):
    B, S, H, D = q.shape
    scale = D ** -0.5
    nb = -(-S // MOBA_BLOCK)
    pad = nb * MOBA_BLOCK - S

    def blocks(x):
        x = jnp.pad(x, ((0, 0), (0, pad), (0, 0), (0, 0)))
        return x.reshape(B, nb, MOBA_BLOCK, H, D).transpose(0, 3, 1, 2, 4)

    kb, vb = blocks(k), blocks(v)
    kmean = jnp.mean(kb.astype(F32), axis=3)
    k_sel_n = min(MOBA_TOPK, nb - 1)
    bi = jnp.arange(B)[:, None, None, None]
    hi = jnp.arange(H)[None, None, :, None]
    blk_ids = jnp.arange(nb)

    def one_query_block(qb):
        p0 = qb * MOBA_Q_BLOCK
        qblk = lax.dynamic_slice_in_dim(q, p0, MOBA_Q_BLOCK, axis=1)
        qpos = p0 + jnp.arange(MOBA_Q_BLOCK)
        own = p0 // MOBA_BLOCK
        k_own = lax.dynamic_index_in_dim(kb, own, axis=2, keepdims=False)
        v_own = lax.dynamic_index_in_dim(vb, own, axis=2, keepdims=False)
        kpos = own * MOBA_BLOCK + jnp.arange(MOBA_BLOCK)
        s_own = jnp.einsum('bqhd,bhtd->bqht', qblk, k_own, preferred_element_type=F32) * scale
        s_own = jnp.where((kpos[None, :] <= qpos[:, None])[None, :, None, :], s_own, -jnp.inf)
        if k_sel_n > 0:
            gate = jnp.einsum('bqhd,bhnd->bqhn', qblk.astype(F32), kmean)
            gate = jnp.where(blk_ids < own, gate, -jnp.inf)
            _, idx = lax.top_k(gate, k_sel_n)
            k_sel = kb[bi, hi, idx]
            v_sel = vb[bi, hi, idx]
            s_sel = jnp.einsum('bqhd,bqhktd->bqhkt', qblk, k_sel, preferred_element_type=F32) * scale
            s_sel = jnp.where((idx < own)[..., None], s_sel, -jnp.inf)
            s_sel = s_sel.reshape(B, MOBA_Q_BLOCK, H, k_sel_n * MOBA_BLOCK)
            probs = jax.nn.softmax(jnp.concatenate([s_sel, s_own], axis=-1), axis=-1).astype(v.dtype)
            p_sel = probs[..., :k_sel_n * MOBA_BLOCK].reshape(B, MOBA_Q_BLOCK, H, k_sel_n, MOBA_BLOCK)
            p_own = probs[..., k_sel_n * MOBA_BLOCK:]
            out = (jnp.einsum('bqhkt,bqhktd->bqhd', p_sel, v_sel, preferred_element_type=F32)
                   + jnp.einsum('bqht,bhtd->bqhd', p_own, v_own, preferred_element_type=F32))
        else:
            p_own = jax.nn.softmax(s_own, axis=-1).astype(v.dtype)
            out = jnp.einsum('bqht,bhtd->bqhd', p_own, v_own, preferred_element_type=F32)
        return out.astype(q.dtype)

    out = lax.map(one_query_block, jnp.arange(S // MOBA_Q_BLOCK))
    return out.transpose(1, 0, 2, 3, 4).reshape(B, S, H, D)


def moba_sample(q, k_new, v_new, pool_k, pool_v, page_table, layer):
    DB, S, H, D = q.shape
    assert S <= PAGE_SIZE
    scale = D ** -0.5
    n_pages = page_table.shape[1]
    ppb = MOBA_BLOCK // PAGE_SIZE
    own = (n_pages * PAGE_SIZE) // MOBA_BLOCK
    n_full = own * ppb
    s_parts, v_parts = [], []
    if own > 0:
        k_full = pool_k[layer, page_table[:, :n_full]]
        kmean = jnp.mean(k_full.astype(F32).reshape(DB, own, ppb, H, PAGE_SIZE, D), axis=(2, 4))
        gate = jnp.einsum('bqhd,bnhd->bqhn', q.astype(F32), kmean)
        k_sel_n = min(MOBA_TOPK, own)
        _, idx = lax.top_k(gate, k_sel_n)
        bi = jnp.arange(DB)[:, None, None, None, None]
        hi = jnp.arange(H)[None, None, :, None, None]
        phys = page_table[bi, idx[..., None] * ppb + jnp.arange(ppb)]
        k_sel = pool_k[layer, phys, hi].reshape(DB, S, H, k_sel_n * MOBA_BLOCK, D)
        v_sel = pool_v[layer, phys, hi].reshape(DB, S, H, k_sel_n * MOBA_BLOCK, D)
        s_parts.append(jnp.einsum('bqhd,bqhtd->bqht', q, k_sel, preferred_element_type=F32) * scale)
        v_parts.append(v_sel.astype(v_new.dtype))
    if n_pages > n_full:
        k_op = pool_k[layer, page_table[:, n_full:]].transpose(0, 2, 1, 3, 4).reshape(DB, H, -1, D)
        v_op = pool_v[layer, page_table[:, n_full:]].transpose(0, 2, 1, 3, 4).reshape(DB, H, -1, D)
        s_parts.append(jnp.einsum('bqhd,bhtd->bqht', q, k_op, preferred_element_type=F32) * scale)
        v_parts.append(jnp.broadcast_to(v_op[:, None], (DB, S) + v_op.shape[1:]).astype(v_new.dtype))
    qpos = jnp.arange(S)
    s_new = jnp.einsum('bqhd,bthd->bqht', q, k_new, preferred_element_type=F32) * scale
    s_new = jnp.where((qpos[None, :] <= qpos[:, None])[None, :, None, :], s_new, -jnp.inf)
    s_parts.append(s_new)
    v_parts.append(jnp.broadcast_to(v_new.transpose(0, 2, 1, 3)[:, None], (DB, S, H, S, D)))
    probs = jax.nn.softmax(jnp.concatenate(s_parts, axis=-1), axis=-1).astype(v_new.dtype)
    vals = jnp.concatenate(v_parts, axis=3)
    out = jnp.einsum('bqht,bqhtd->bqhd', probs, vals, preferred_element_type=F32)
    return out.astype(q.dtype)


def memory_kv(mem, w_norm, w_kv):
    B, M, _ = mem.shape
    k, v = jnp.split(rms_norm(mem, w_norm) @ w_kv, 2, axis=-1)
    return k.reshape(B, M, CROSS_HEADS, HEAD_DIM), v.reshape(B, M, CROSS_HEADS, HEAD_DIM)


def cross_attend(hn, mem_k, mem_v, w_q, w_o):
    B, L, _ = hn.shape
    q = (hn @ w_q).reshape(B, L, CROSS_HEADS, HEAD_DIM)
    s = jnp.einsum('blhd,bmhd->blhm', q, mem_k, preferred_element_type=F32) * HEAD_DIM ** -0.5
    p = jax.nn.softmax(s, axis=-1).astype(mem_v.dtype)
    o = jnp.einsum('blhm,bmhd->blhd', p, mem_v, preferred_element_type=F32).astype(hn.dtype)
    return o.reshape(B, L, CROSS_WIDTH) @ w_o


def hybrid_layer(x, pos, conv_buf, gdn_s0, mem_k, mem_v, moba_attend,
                 w_ffn1_norm, w_ffn1_gu, w_ffn1_down, w_mix_norm, w_in, gdn_conv_w, gdn_a_log,
                 gdn_dt_bias, gdn_out_norm, w_out, w_cross_norm, w_cross_q, w_cross_out,
                 w_ffn2_norm, w_ffn2_gu, w_ffn2_down):
    B, L, _ = x.shape
    h = x + 0.5 * swiglu(rms_norm(x, w_ffn1_norm), w_ffn1_gu, w_ffn1_down)
    cols = rms_norm(h, w_mix_norm) @ w_in
    o_gdn, gdn_s, conv_s = gdn_mixer(cols[..., :GDN_COLS], conv_buf, gdn_s0, gdn_conv_w,
                                     gdn_a_log, gdn_dt_bias, gdn_out_norm)
    mq, mk, mv = jnp.split(cols[..., GDN_COLS:], 3, axis=-1)
    shp = (B, L, MOBA_HEADS, HEAD_DIM)
    mq = rope_partial(mq.reshape(shp), pos)
    mk = rope_partial(mk.reshape(shp), pos)
    mv = mv.reshape(shp)
    o_moba = moba_attend(mq, mk, mv).reshape(B, L, MOBA_WIDTH)
    h = h + jnp.concatenate([o_gdn, o_moba], axis=-1) @ w_out
    h = h + cross_attend(rms_norm(h, w_cross_norm), mem_k, mem_v, w_cross_q, w_cross_out)
    h = h + 0.5 * swiglu(rms_norm(h, w_ffn2_norm), w_ffn2_gu, w_ffn2_down)
    return h, mk.transpose(0, 2, 1, 3), mv.transpose(0, 2, 1, 3), gdn_s, conv_s


def setup_inputs(seed: int = 0) -> dict:
    key = jax.random.key(seed)
    ks = jax.random.split(key, 32)
    n_pages = PAST_LEN // PAGE_SIZE
    n_pool = (DEC_BATCH * n_pages * 5) // 4

    def nrm(k, shape, scale=1.0):
        return jax.random.normal(k, shape, F32) * scale

    def gain(k, shape):
        return 1.0 + nrm(k, shape, 0.02)

    perm = jax.random.permutation(ks[4], n_pool)
    page_table = perm[:DEC_BATCH * n_pages].reshape(DEC_BATCH, n_pages).astype(jnp.int32)
    dt = jax.random.uniform(ks[20], (DEPTH, GDN_HEADS), F32, minval=1e-3, maxval=1e-1)
    return {
        'x_prompt': nrm(ks[0], (BATCH, SEQ, D_MODEL)),
        'x_sample': nrm(ks[1], (DEC_BATCH, DEC_SEQ, D_MODEL)),
        'cache_moba_k': nrm(ks[2], (DEPTH, n_pool, MOBA_HEADS, PAGE_SIZE, HEAD_DIM)),
        'cache_moba_v': nrm(ks[3], (DEPTH, n_pool, MOBA_HEADS, PAGE_SIZE, HEAD_DIM)),
        'page_table': page_table,
        'state_gdn': nrm(ks[5], (DEPTH, DEC_BATCH, GDN_HEADS, HEAD_DIM, HEAD_DIM), 0.05),
        'state_conv': nrm(ks[6], (DEPTH, DEC_BATCH, CONV_WIDTH - 1, GDN_CONV_CH)),
        'cache_mem_k': nrm(ks[7], (DEPTH, DEC_BATCH, MEM_TOKENS, CROSS_HEADS, HEAD_DIM)),
        'cache_mem_v': nrm(ks[8], (DEPTH, DEC_BATCH, MEM_TOKENS, CROSS_HEADS, HEAD_DIM)),
        'mem_prompt': nrm(ks[9], (BATCH, MEM_TOKENS, D_MODEL)),
        'w_ffn1_norm': gain(ks[10], (DEPTH, D_MODEL)),
        'w_ffn1_gu': nrm(ks[11], (DEPTH, D_MODEL, 2 * D_FF), D_MODEL ** -0.5),
        'w_ffn1_down': nrm(ks[12], (DEPTH, D_FF, D_MODEL), D_FF ** -0.5),
        'w_mix_norm': gain(ks[13], (DEPTH, D_MODEL)),
        'w_in': nrm(ks[14], (DEPTH, D_MODEL, IN_COLS), D_MODEL ** -0.5),
        'gdn_conv_w': nrm(ks[15], (DEPTH, CONV_WIDTH, GDN_CONV_CH), CONV_WIDTH ** -0.5),
        'gdn_a_log': jnp.log(jax.random.uniform(ks[16], (DEPTH, GDN_HEADS), F32, minval=1.0, maxval=16.0)),
        'gdn_dt_bias': jnp.log(jnp.expm1(dt)),
        'gdn_out_norm': gain(ks[17], (DEPTH, HEAD_DIM)),
        'w_out': nrm(ks[18], (DEPTH, MIX_WIDTH, D_MODEL), MIX_WIDTH ** -0.5),
        'w_cross_norm': gain(ks[19], (DEPTH, D_MODEL)),
        'w_cross_q': nrm(ks[21], (DEPTH, D_MODEL, CROSS_WIDTH), D_MODEL ** -0.5),
        'w_cross_out': nrm(ks[22], (DEPTH, CROSS_WIDTH, D_MODEL), CROSS_WIDTH ** -0.5),
        'w_mem_norm': gain(ks[23], (DEPTH, D_MODEL)),
        'w_mem_kv': nrm(ks[24], (DEPTH, D_MODEL, 2 * CROSS_WIDTH), D_MODEL ** -0.5),
        'w_ffn2_norm': gain(ks[25], (DEPTH, D_MODEL)),
        'w_ffn2_gu': nrm(ks[26], (DEPTH, D_MODEL, 2 * D_FF), D_MODEL ** -0.5),
        'w_ffn2_down': nrm(ks[27], (DEPTH, D_FF, D_MODEL), D_FF ** -0.5),
        'w_final_norm': gain(ks[28], (D_MODEL,)),
    }


def reference(x_prompt, x_sample, cache_moba_k, cache_moba_v, page_table, state_gdn, state_conv,
              cache_mem_k, cache_mem_v, mem_prompt, w_ffn1_norm, w_ffn1_gu, w_ffn1_down, w_mix_norm,
              w_in, gdn_conv_w, gdn_a_log, gdn_dt_bias, gdn_out_norm, w_out, w_cross_norm, w_cross_q,
              w_cross_out, w_mem_norm, w_mem_kv, w_ffn2_norm, w_ffn2_gu, w_ffn2_down, w_final_norm):
    B, S, _ = x_prompt.shape
    DB, DS, _ = x_sample.shape
    pos_p = jnp.arange(S)
    pos_s = PAST_LEN + jnp.arange(DS)
    hp, hs = x_prompt, x_sample
    p_k, p_v, p_g, p_c, p_mk, p_mv = [], [], [], [], [], []
    s_k, s_v, s_g, s_c = [], [], [], []
    for layer in range(DEPTH):
        lw = (w_ffn1_norm[layer], w_ffn1_gu[layer], w_ffn1_down[layer], w_mix_norm[layer], w_in[layer],
              gdn_conv_w[layer], gdn_a_log[layer], gdn_dt_bias[layer], gdn_out_norm[layer], w_out[layer],
              w_cross_norm[layer], w_cross_q[layer], w_cross_out[layer], w_ffn2_norm[layer],
              w_ffn2_gu[layer], w_ffn2_down[layer])
        mem_k_p, mem_v_p = memory_kv(mem_prompt, w_mem_norm[layer], w_mem_kv[layer])
        conv0 = jnp.zeros((B, CONV_WIDTH - 1, GDN_CONV_CH), x_prompt.dtype)
        gdn0 = jnp.zeros((B, GDN_HEADS, HEAD_DIM, HEAD_DIM), state_gdn.dtype)
        hp, pk, pv, pg, pc = hybrid_layer(hp, pos_p, conv0, gdn0, mem_k_p, mem_v_p, moba_prompt, *lw)
        sample_attend = functools.partial(moba_sample, pool_k=cache_moba_k, pool_v=cache_moba_v,
                                          page_table=page_table, layer=layer)
        hs, sk, sv, sg, sc = hybrid_layer(hs, pos_s, state_conv[layer], state_gdn[layer],
                                          cache_mem_k[layer], cache_mem_v[layer], sample_attend, *lw)
        p_k.append(pk); p_v.append(pv); p_g.append(pg); p_c.append(pc)
        p_mk.append(mem_k_p); p_mv.append(mem_v_p)
        s_k.append(sk); s_v.append(sv); s_g.append(sg); s_c.append(sc)
    y_prompt = rms_norm(hp, w_final_norm)
    y_sample = rms_norm(hs, w_final_norm)
    return (y_prompt, y_sample, jnp.stack(p_k), jnp.stack(p_v), jnp.stack(p_g), jnp.stack(p_c),
            jnp.stack(p_mk), jnp.stack(p_mv), jnp.stack(s_k), jnp.stack(s_v), jnp.stack(s_g), jnp.stack(s_c))
```

```python
import functools

import jax
import jax.numpy as jnp
from jax import lax
from jax.experimental import pallas as pl
from jax.experimental.pallas import tpu as pltpu

F32 = jnp.float32
BF16 = jnp.bfloat16
HIGHEST = lax.Precision.HIGHEST

HEAD_DIM = 128
GDN_HEADS = 8
MOBA_HEADS = 8
GDN_WIDTH = GDN_HEADS * HEAD_DIM
MOBA_WIDTH = MOBA_HEADS * HEAD_DIM
CONV_WIDTH = 4
GDN_CONV_CH = 3 * GDN_WIDTH
GDN_CHUNK = 64
MOBA_BLOCK = 256
MOBA_TOPK = 3
PAGE_SIZE = 128
ROPE_DIM = HEAD_DIM // 4
ROPE_THETA = 500000.0
CROSS_HEADS = 4
CROSS_WIDTH = CROSS_HEADS * HEAD_DIM
NORM_EPS = 1e-6

LANES = 128
MASK_VALUE = -1e30
VMEM_LIMIT_BYTES = 56 * 1024 * 1024

_NT = (((1,), (1,)), ((), ()))
_TN = (((0,), (0,)), ((), ()))


def _params(*sem):
    return pltpu.CompilerParams(dimension_semantics=sem, vmem_limit_bytes=VMEM_LIMIT_BYTES)


def _rms(x, g):
    return x * lax.rsqrt(jnp.mean(x * x, axis=-1, keepdims=True) + NORM_EPS) * g


def _silu(x):
    return x * jax.nn.sigmoid(x)


def _dot(a, b):
    return jnp.dot(a, b, preferred_element_type=F32)


def _dot_nt(a, b):
    return lax.dot_general(a, b, _NT, preferred_element_type=F32)


def _dot_f32(a, b):
    return jnp.dot(a, b, precision=HIGHEST, preferred_element_type=F32)


def _col(x, c, width):
    return jnp.broadcast_to(x[:, c:c + 1], (x.shape[0], width))


def _tile(m, pref):
    return pref if m % pref == 0 else m


def _rms_matmul_kernel(x_ref, g_ref, w_ref, o_ref, xn_ref):
    @pl.when(pl.program_id(1) == 0)
    def _():
        xn_ref[...] = _rms(x_ref[...], g_ref[...]).astype(BF16)

    o_ref[...] = _dot(xn_ref[...], w_ref[...])


def _rms_matmul(x, g, w, *, tm, tn):
    m, d = x.shape
    n = w.shape[1]
    return pl.pallas_call(
        _rms_matmul_kernel,
        grid=(m // tm, n // tn),
        in_specs=[pl.BlockSpec((tm, d), lambda i, j: (i, 0)),
                  pl.BlockSpec((1, d), lambda i, j: (0, 0)),
                  pl.BlockSpec((d, tn), lambda i, j: (0, j))],
        out_specs=pl.BlockSpec((tm, tn), lambda i, j: (i, j)),
        out_shape=jax.ShapeDtypeStruct((m, n), F32),
        scratch_shapes=[pltpu.VMEM((tm, d), BF16)],
        compiler_params=_params("parallel", "arbitrary"),
        name="rms_matmul",
    )(x, g, w)


def _ffn_kernel(x_ref, g_ref, wg_ref, wu_ref, wd_ref, fg_ref, o_ref, xn_ref, acc_ref, *, final_norm):
    j = pl.program_id(1)

    @pl.when(j == 0)
    def _():
        xn_ref[...] = _rms(x_ref[...], g_ref[...]).astype(BF16)
        acc_ref[...] = jnp.zeros_like(acc_ref)

    xn = xn_ref[...]
    act = _silu(_dot(xn, wg_ref[...])) * _dot(xn, wu_ref[...])
    acc_ref[...] += _dot(act.astype(BF16), wd_ref[...])

    @pl.when(j == pl.num_programs(1) - 1)
    def _():
        h = x_ref[...] + 0.5 * acc_ref[...]
        if final_norm:
            h = _rms(h, fg_ref[...])
        o_ref[...] = h


def _ffn(x, g, w_gu, w_down, fg, *, final_norm, tm, tf):
    m, d = x.shape
    f = w_down.shape[0]
    nf = f // tf
    return pl.pallas_call(
        functools.partial(_ffn_kernel, final_norm=final_norm),
        grid=(m // tm, nf),
        in_specs=[pl.BlockSpec((tm, d), lambda i, j: (i, 0)),
                  pl.BlockSpec((1, d), lambda i, j: (0, 0)),
                  pl.BlockSpec((d, tf), lambda i, j: (0, j)),
                  pl.BlockSpec((d, tf), lambda i, j: (0, nf + j)),
                  pl.BlockSpec((tf, d), lambda i, j: (j, 0)),
                  pl.BlockSpec((1, d), lambda i, j: (0, 0))],
        out_specs=pl.BlockSpec((tm, d), lambda i, j: (i, 0)),
        out_shape=jax.ShapeDtypeStruct((m, d), F32),
        scratch_shapes=[pltpu.VMEM((tm, d), BF16), pltpu.VMEM((tm, d), F32)],
        compiler_params=_params("parallel", "arbitrary"),
        name="ffn",
    )(x, g, w_gu, w_gu, w_down, fg)


def _proj_res_kernel(*refs):
    res_ref, o_ref = refs[0], refs[-1]
    pairs = refs[1:-1]
    acc = res_ref[...]
    for a_ref, w_ref in zip(pairs[0::2], pairs[1::2]):
        acc = acc + _dot(a_ref[...].astype(BF16), w_ref[...])
    o_ref[...] = acc


def _proj_res(res, pairs, *, tm):
    m, d = res.shape
    in_specs = [pl.BlockSpec((tm, d), lambda i: (i, 0))]
    args = [res]
    for a, w in pairs:
        in_specs.append(pl.BlockSpec((tm, a.shape[1]), lambda i: (i, 0)))
        in_specs.append(pl.BlockSpec(w.shape, lambda i: (0, 0)))
        args += [a, w]
    return pl.pallas_call(
        _proj_res_kernel,
        grid=(m // tm,),
        in_specs=in_specs,
        out_specs=pl.BlockSpec((tm, d), lambda i: (i, 0)),
        out_shape=jax.ShapeDtypeStruct((m, d), F32),
        compiler_params=_params("parallel"),
        name="proj_res",
    )(*args)


def _cross_kernel(h_ref, g_ref, wq_ref, mk_ref, mv_ref, wo_ref, o_ref):
    h = h_ref[...]
    q = _dot(_rms(h, g_ref[...]).astype(BF16), wq_ref[...])
    outs = []
    for hd in range(CROSS_HEADS):
        sl = slice(hd * HEAD_DIM, (hd + 1) * HEAD_DIM)
        s = _dot_nt(q[:, sl].astype(BF16), mk_ref[:, sl]) * HEAD_DIM ** -0.5
        e = jnp.exp(s - jnp.max(s, axis=-1, keepdims=True))
        p = e / jnp.sum(e, axis=-1, keepdims=True)
        outs.append(_dot(p.astype(BF16), mv_ref[:, sl]).astype(BF16))
    o_ref[...] = h + _dot(jnp.concatenate(outs, axis=-1), wo_ref[...])


def _cross_prompt(h, g, wq, mem_k, mem_v, wo, *, tm):
    m, d = h.shape
    full = lambda a: pl.BlockSpec(a.shape, lambda i: (0, 0))
    return pl.pallas_call(
        _cross_kernel,
        grid=(m // tm,),
        in_specs=[pl.BlockSpec((tm, d), lambda i: (i, 0)), full(g), full(wq), full(mem_k), full(mem_v), full(wo)],
        out_specs=pl.BlockSpec((tm, d), lambda i: (i, 0)),
        out_shape=jax.ShapeDtypeStruct((m, d), F32),
        compiler_params=_params("parallel"),
        name="cross_prompt",
    )(h, g, wq, mem_k, mem_v, wo)


def _gdn_prompt_kernel(qkv_ref, z_ref, ba_ref, cw_ref, alog_ref, dtb_ref, onorm_ref,
                       o_ref, sfin_ref, convn_ref, xp_ref, y_ref, s_ref):
    c = GDN_CHUNK
    i = pl.program_id(0)

    @pl.when(i == 0)
    def _():
        xp_ref[0:8, :] = jnp.zeros((8, GDN_CONV_CH), F32)
        s_ref[...] = jnp.zeros_like(s_ref)

    xp_ref[8:8 + c, :] = qkv_ref[...]
    y = cw_ref[0:1, :] * xp_ref[5:5 + c, :]
    for j in range(1, CONV_WIDTH):
        y = y + cw_ref[j:j + 1, :] * xp_ref[5 + j:5 + j + c, :]
    y_ref[...] = _silu(y)
    tail = xp_ref[c:c + 8, :]
    xp_ref[0:8, :] = tail
    convn_ref[...] = tail

    ba = ba_ref[...]
    beta = jax.nn.sigmoid(ba)
    g = -jnp.exp(alog_ref[...]) * jax.nn.softplus(ba + dtb_ref[...])
    row = lax.broadcasted_iota(jnp.int32, (c, c), 0)
    colv = lax.broadcasted_iota(jnp.int32, (c, c), 1)
    causal = row >= colv
    strict = row > colv
    eye = (row == colv).astype(F32)
    gc = _dot_f32(causal.astype(F32), g)
    gc_t = gc.T
    eg = jnp.exp(gc)
    g_last = gc[c - 1:c, :]
    eg_last = jnp.exp(g_last)
    k_decay = jnp.exp(g_last - gc)

    for h in range(GDN_HEADS):
        sl = slice(h * HEAD_DIM, (h + 1) * HEAD_DIM)
        qh = y_ref[:, h * HEAD_DIM:(h + 1) * HEAD_DIM]
        kh = y_ref[:, GDN_WIDTH + h * HEAD_DIM:GDN_WIDTH + (h + 1) * HEAD_DIM]
        vh = y_ref[:, 2 * GDN_WIDTH + h * HEAD_DIM:2 * GDN_WIDTH + (h + 1) * HEAD_DIM]
        qn = qh * lax.rsqrt(jnp.sum(qh * qh, axis=-1, keepdims=True) + NORM_EPS) * HEAD_DIM ** -0.5
        kn = kh * lax.rsqrt(jnp.sum(kh * kh, axis=-1, keepdims=True) + NORM_EPS)
        gh = GDN_HEADS + h
        b_col = _col(beta, h, HEAD_DIM)
        eg_col = _col(eg, gh, HEAD_DIM)
        diff = _col(gc, gh, c) - jnp.broadcast_to(gc_t[gh:gh + 1, :], (c, c))
        decay = jnp.exp(jnp.where(causal, diff, -jnp.inf))
        kb = kn * b_col
        kn16 = kn.astype(BF16)
        a = _dot_nt(kb.astype(BF16), kn16) * jnp.where(strict, decay, 0.0)
        p = -a
        t = eye + p
        for _ in range(5):
            p = _dot_f32(p, p)
            t = t + _dot_f32(t, p)
        t16 = t.astype(BF16)
        u = _dot(t16, (vh * b_col).astype(BF16))
        w = _dot(t16, (kb * eg_col).astype(BF16))
        s = s_ref[h]
        s16 = s.astype(BF16)
        v_new = u - _dot(w.astype(BF16), s16)
        v_new16 = v_new.astype(BF16)
        qn16 = qn.astype(BF16)
        attn = _dot_nt(qn16, kn16) * decay
        o = _dot((qn * eg_col).astype(BF16), s16) + _dot(attn.astype(BF16), v_new16)
        kd = (kn * _col(k_decay, gh, HEAD_DIM)).astype(BF16)
        s_ref[h] = s * eg_last[:, gh:gh + 1] + lax.dot_general(kd, v_new16, _TN, preferred_element_type=F32)
        o = o * lax.rsqrt(jnp.mean(o * o, axis=-1, keepdims=True) + NORM_EPS) * onorm_ref[...]
        o_ref[:, sl] = o * _silu(z_ref[:, sl])

    @pl.when(i == pl.num_programs(0) - 1)
    def _():
        sfin_ref[...] = s_ref[...]


def _gdn_prompt(cols, ba, conv_w, alog_row, dtb_row, onorm):
    s_len = cols.shape[0]
    c = GDN_CHUNK
    assert s_len % c == 0
    full = lambda a: pl.BlockSpec(a.shape, lambda i: (0,) * a.ndim)
    return pl.pallas_call(
        _gdn_prompt_kernel,
        grid=(s_len // c,),
        in_specs=[pl.BlockSpec((c, GDN_CONV_CH), lambda i: (i, 0)),
                  pl.BlockSpec((c, GDN_WIDTH), lambda i: (i, GDN_CONV_CH // GDN_WIDTH)),
                  pl.BlockSpec((c, LANES), lambda i: (i, 0)),
                  full(conv_w), full(alog_row), full(dtb_row), full(onorm)],
        out_specs=[pl.BlockSpec((c, GDN_WIDTH), lambda i: (i, 0)),
                   pl.BlockSpec((GDN_HEADS, HEAD_DIM, HEAD_DIM), lambda i: (0, 0, 0)),
                   pl.BlockSpec((8, GDN_CONV_CH), lambda i: (0, 0))],
        out_shape=[jax.ShapeDtypeStruct((s_len, GDN_WIDTH), F32),
                   jax.ShapeDtypeStruct((GDN_HEADS, HEAD_DIM, HEAD_DIM), F32),
                   jax.ShapeDtypeStruct((8, GDN_CONV_CH), F32)],
        scratch_shapes=[pltpu.VMEM((c + 8, GDN_CONV_CH), F32),
                        pltpu.VMEM((c, GDN_CONV_CH), F32),
                        pltpu.VMEM((GDN_HEADS, HEAD_DIM, HEAD_DIM), F32)],
        compiler_params=_params("arbitrary"),
        name="gdn_prompt",
    )(cols, cols, ba, conv_w, alog_row, dtb_row, onorm)


def _gdn_sample_kernel(x_ref, cs_ref, z_ref, ba_ref, cw_ref, alog_ref, dtb_ref, onorm_ref, s0_ref,
                       o_ref, s_ref, convn_ref):
    x = x_ref[0]
    cs = cs_ref[0]
    y = cw_ref[CONV_WIDTH - 1:CONV_WIDTH, :] * x
    for j in range(CONV_WIDTH - 1):
        y = y + cw_ref[j:j + 1, :] * cs[j:j + 1, :]
    y = _silu(y)
    convn_ref[0] = jnp.concatenate([cs[1:CONV_WIDTH - 1, :], x], axis=0)

    ba = ba_ref[0]
    beta = jax.nn.sigmoid(ba)
    eg = jnp.exp(-jnp.exp(alog_ref[...]) * jax.nn.softplus(ba + dtb_ref[...]))
    z = z_ref[0]

    def head_rows(base):
        return jnp.concatenate([y[:, base + h * HEAD_DIM:base + (h + 1) * HEAD_DIM] for h in range(GDN_HEADS)], axis=0)

    q8 = head_rows(0)
    k8 = head_rows(GDN_WIDTH)
    v8 = head_rows(2 * GDN_WIDTH)
    q8 = q8 * lax.rsqrt(jnp.sum(q8 * q8, axis=-1, keepdims=True) + NORM_EPS) * HEAD_DIM ** -0.5
    k8 = k8 * lax.rsqrt(jnp.sum(k8 * k8, axis=-1, keepdims=True) + NORM_EPS)
    qk8 = jnp.sum(q8 * k8, axis=-1, keepdims=True)
    q_t = q8.T
    k_t = k8.T
    outs = []
    for h in range(GDN_HEADS):
        s = s0_ref[0, h]
        k_col = _col(k_t, h, HEAD_DIM)
        q_col = _col(q_t, h, HEAD_DIM)
        e = eg[:, GDN_HEADS + h:GDN_HEADS + h + 1]
        k_s = jnp.sum(k_col * s, axis=0, keepdims=True)
        q_s = jnp.sum(q_col * s, axis=0, keepdims=True)
        v_new = beta[:, h:h + 1] * (v8[h:h + 1, :] - e * k_s)
        o = e * q_s + qk8[h:h + 1, :] * v_new
        s_ref[0, h] = s * e + k_col * v_new
        o = o * lax.rsqrt(jnp.mean(o * o, axis=-1, keepdims=True) + NORM_EPS) * onorm_ref[...]
        outs.append(o * _silu(z[:, h * HEAD_DIM:(h + 1) * HEAD_DIM]))
    o_ref[0] = jnp.concatenate(outs, axis=-1)


def _gdn_sample(x, conv_state, z, ba, conv_w, alog_row, dtb_row, onorm, s0):
    nb = x.shape[0]
    full = lambda a: pl.BlockSpec(a.shape, lambda b: (0,) * a.ndim)
    row = lambda a: pl.BlockSpec((1,) + a.shape[1:], lambda b: (b,) + (0,) * (a.ndim - 1))
    return pl.pallas_call(
        _gdn_sample_kernel,
        grid=(nb,),
        in_specs=[row(x), row(conv_state), row(z), row(ba), full(conv_w), full(alog_row), full(dtb_row),
                  full(onorm), row(s0)],
        out_specs=[pl.BlockSpec((1, 1, GDN_WIDTH), lambda b: (b, 0, 0)), row(s0), row(conv_state)],
        out_shape=[jax.ShapeDtypeStruct((nb, 1, GDN_WIDTH), F32),
                   jax.ShapeDtypeStruct(s0.shape, F32),
                   jax.ShapeDtypeStruct(conv_state.shape, F32)],
        compiler_params=_params("parallel"),
        name="gdn_sample",
    )(x, conv_state, z, ba, conv_w, alog_row, dtb_row, onorm, s0)


def _rope(x, cos, sin_signed):
    lane = lax.broadcasted_iota(jnp.int32, x.shape, x.ndim - 1)
    half = ROPE_DIM // 2
    partner = jnp.where(lane < half, pltpu.roll(x, LANES - half, x.ndim - 1), pltpu.roll(x, half, x.ndim - 1))
    return x * cos + partner * sin_signed


def _moba_prep_kernel(q_ref, k_ref, v_ref, cos_ref, sin_ref,
                      q16_ref, kf_ref, k16_ref, vf_ref, vt16_ref, bias_ref, kmean_ref, *, n_sel):
    i = pl.program_id(0)
    nb = kmean_ref.shape[1]
    t = q_ref.shape[0]

    @pl.when(i == 0)
    def _():
        kmean_ref[...] = jnp.zeros_like(kmean_ref)

    cos = cos_ref[...]
    sin = sin_ref[...]
    blk = lax.broadcasted_iota(jnp.int32, (nb, t), 0)
    past = blk < i
    for h in range(MOBA_HEADS):
        sl = slice(h * HEAD_DIM, (h + 1) * HEAD_DIM)
        q = _rope(q_ref[:, sl], cos, sin)
        k = _rope(k_ref[:, sl], cos, sin)
        v = v_ref[:, sl]
        q16_ref[h] = (q * HEAD_DIM ** -0.5).astype(BF16)
        kf_ref[h] = k
        k16_ref[h, 0] = k.astype(BF16)
        vf_ref[h] = v
        vt16_ref[h, 0] = v.T.astype(BF16)
        gate = lax.dot_general(kmean_ref[h], q, _NT, precision=HIGHEST, preferred_element_type=F32)
        gate = jnp.where(past, gate, -jnp.inf)
        chosen = jnp.zeros((nb, t), F32)
        for _ in range(n_sel):
            top = jnp.max(gate, axis=0, keepdims=True)
            first = jnp.min(jnp.where(gate == top, blk, nb), axis=0, keepdims=True)
            pick = (blk == first) & (top > -jnp.inf)
            chosen = jnp.where(pick, 1.0, chosen)
            gate = jnp.where(pick, -jnp.inf, gate)
        bias_ref[h] = jnp.where(chosen > 0.0, 0.0, MASK_VALUE)
        kmean_ref[h, pl.ds(i, 1), :] = jnp.mean(k, axis=0, keepdims=True)


def _moba_prep(cols, cos, sin, col0):
    s_len = cols.shape[0]
    t = MOBA_BLOCK
    assert s_len % t == 0
    nb = s_len // t
    hh = MOBA_HEADS
    cb = col0 // MOBA_WIDTH
    n_sel = min(MOBA_TOPK, nb - 1)
    return pl.pallas_call(
        functools.partial(_moba_prep_kernel, n_sel=n_sel),
        grid=(nb,),
        in_specs=[pl.BlockSpec((t, MOBA_WIDTH), lambda i: (i, cb)),
                  pl.BlockSpec((t, MOBA_WIDTH), lambda i: (i, cb + 1)),
                  pl.BlockSpec((t, MOBA_WIDTH), lambda i: (i, cb + 2)),
                  pl.BlockSpec((t, LANES), lambda i: (i, 0)),
                  pl.BlockSpec((t, LANES), lambda i: (i, 0))],
        out_specs=[pl.BlockSpec((hh, t, HEAD_DIM), lambda i: (0, i, 0)),
                   pl.BlockSpec((hh, t, HEAD_DIM), lambda i: (0, i, 0)),
                   pl.BlockSpec((hh, 1, t, HEAD_DIM), lambda i: (0, i, 0, 0)),
                   pl.BlockSpec((hh, t, HEAD_DIM), lambda i: (0, i, 0)),
                   pl.BlockSpec((hh, 1, HEAD_DIM, t), lambda i: (0, i, 0, 0)),
                   pl.BlockSpec((hh, nb, t), lambda i: (0, 0, i))],
        out_shape=[jax.ShapeDtypeStruct((hh, s_len, HEAD_DIM), BF16),
                   jax.ShapeDtypeStruct((hh, s_len, HEAD_DIM), F32),
                   jax.ShapeDtypeStruct((hh, nb, t, HEAD_DIM), BF16),
                   jax.ShapeDtypeStruct((hh, s_len, HEAD_DIM), F32),
                   jax.ShapeDtypeStruct((hh, nb, HEAD_DIM, t), BF16),
                   jax.ShapeDtypeStruct((hh, nb, s_len), F32)],
        scratch_shapes=[pltpu.VMEM((hh, nb, HEAD_DIM), F32)],
        compiler_params=_params("arbitrary"),
        name="moba_prep",
    )(cols, cols, cols, cos, sin)


def _moba_attn_kernel(q_ref, k_ref, vt_ref, bias_ref, o_ref):
    tq = q_ref.shape[1]
    tk = k_ref.shape[2]
    qi = pl.program_id(1)
    own = (qi * tq) // tk
    q = q_ref[0]

    def tile(kb, carry, s_mask):
        m, l, acc = carry
        s = s_mask(_dot_nt(k_ref[0, kb], q))
        m_new = jnp.maximum(m, jnp.max(s, axis=0, keepdims=True))
        alpha = jnp.exp(m - m_new)
        p = jnp.exp(s - m_new)
        l = alpha * l + jnp.sum(p, axis=0, keepdims=True)
        acc = acc * alpha + _dot(vt_ref[0, kb], p.astype(BF16))
        return m_new, l, acc

    def own_mask(s):
        kpos = own * tk + lax.broadcasted_iota(jnp.int32, (tk, tq), 0)
        qpos = qi * tq + lax.broadcasted_iota(jnp.int32, (tk, tq), 1)
        return jnp.where(kpos <= qpos, s, MASK_VALUE)

    init = (jnp.full((1, tq), MASK_VALUE, F32), jnp.zeros((1, tq), F32), jnp.zeros((HEAD_DIM, tq), F32))
    carry = tile(own, init, own_mask)

    def past_tile(kb, carry):
        return tile(kb, carry, lambda s: s + bias_ref[0, pl.ds(kb, 1), :])

    m, l, acc = lax.fori_loop(0, own, past_tile, carry)
    o_ref[...] = (acc / l).T


def _moba_attn(q16, k16, vt16, bias, *, tq):
    hh, s_len, d = q16.shape
    nb, tk = k16.shape[1], k16.shape[2]
    return pl.pallas_call(
        _moba_attn_kernel,
        grid=(hh, s_len // tq),
        in_specs=[pl.BlockSpec((1, tq, d), lambda h, i: (h, i, 0)),
                  pl.BlockSpec((1, nb, tk, d), lambda h, i: (h, 0, 0, 0)),
                  pl.BlockSpec((1, nb, d, tk), lambda h, i: (h, 0, 0, 0)),
                  pl.BlockSpec((1, nb, tq), lambda h, i: (h, 0, i))],
        out_specs=pl.BlockSpec((tq, d), lambda h, i: (i, h)),
        out_shape=jax.ShapeDtypeStruct((s_len, hh * d), F32),
        compiler_params=_params("parallel", "arbitrary"),
        name="moba_attn",
    )(q16, k16, vt16, bias)


_PAGES_PER_STEP = 8


def _moba_kmean_kernel(pt_ref, *refs):
    del pt_ref
    o_ref = refs[-1]
    ppb = MOBA_BLOCK // PAGE_SIZE
    for blk in range(_PAGES_PER_STEP // ppb):
        tot = jnp.sum(refs[blk * ppb][0, 0], axis=1)
        for r in range(1, ppb):
            tot = tot + jnp.sum(refs[blk * ppb + r][0, 0], axis=1)
        o_ref[0, blk] = tot * (1.0 / MOBA_BLOCK)


def _moba_kmean(page_table, pool_k):
    nbat, n_pages = page_table.shape
    _, _, hh, ps, d = pool_k.shape
    ppb = MOBA_BLOCK // PAGE_SIZE
    assert n_pages % _PAGES_PER_STEP == 0
    steps = n_pages // _PAGES_PER_STEP
    in_specs = [pl.BlockSpec((1, 1, hh, ps, d), functools.partial(lambda b, c, pt, r: (0, pt[b, c * _PAGES_PER_STEP + r], 0, 0, 0), r=r))
                for r in range(_PAGES_PER_STEP)]
    gs = pltpu.PrefetchScalarGridSpec(
        num_scalar_prefetch=1, grid=(nbat, steps), in_specs=in_specs,
        out_specs=pl.BlockSpec((1, _PAGES_PER_STEP // ppb, hh, d), lambda b, c, pt: (b, c, 0, 0)))
    return pl.pallas_call(
        _moba_kmean_kernel, grid_spec=gs,
        out_shape=jax.ShapeDtypeStruct((nbat, n_pages // ppb, hh, d), F32),
        compiler_params=_params("parallel", "arbitrary"),
        name="moba_kmean",
    )(page_table, *([pool_k] * _PAGES_PER_STEP))


def _moba_sample_gate_kernel(q_ref, k_ref, cos_ref, sin_ref, km_ref, qr_ref, kr_ref, idx_ref, *, n_sel):
    q = _rope(q_ref[0], cos_ref[...], sin_ref[...])
    kr_ref[0] = _rope(k_ref[0], cos_ref[...], sin_ref[...])
    qr_ref[0] = q
    gate = jnp.sum(km_ref[0] * q, axis=-1)
    nb = gate.shape[0]
    blk = lax.broadcasted_iota(jnp.int32, gate.shape, 0)
    rows = []
    for _ in range(n_sel):
        top = jnp.max(gate, axis=0, keepdims=True)
        first = jnp.min(jnp.where(gate == top, blk, nb), axis=0, keepdims=True)
        rows.append(first)
        gate = jnp.where(blk == first, -jnp.inf, gate)
    idx_ref[0] = jnp.concatenate(rows, axis=0)


def _moba_sample_gate(q, k, cos, sin, kmean, n_sel):
    nbat, hh, d = q.shape
    nb = kmean.shape[1]
    row = lambda a: pl.BlockSpec((1,) + a.shape[1:], lambda b: (b,) + (0,) * (a.ndim - 1))
    full = lambda a: pl.BlockSpec(a.shape, lambda b: (0,) * a.ndim)
    return pl.pallas_call(
        functools.partial(_moba_sample_gate_kernel, n_sel=n_sel),
        grid=(nbat,),
        in_specs=[row(q), row(k), full(cos), full(sin), row(kmean)],
        out_specs=[row(q), row(k), pl.BlockSpec((1, n_sel, hh), lambda b: (b, 0, 0))],
        out_shape=[jax.ShapeDtypeStruct(q.shape, F32), jax.ShapeDtypeStruct(k.shape, F32),
                   jax.ShapeDtypeStruct((nbat, n_sel, hh), jnp.int32)],
        compiler_params=_params("parallel"),
        name="moba_sample_gate",
    )(q, k, cos, sin, kmean)


def _moba_sample_attn_kernel(pt_ref, idx_ref, q_ref, kn_ref, vn_ref, *refs):
    del pt_ref, idx_ref
    o_ref = refs[-1]
    n = (len(refs) - 1) // 2
    k_refs, v_refs = refs[:n], refs[n:2 * n]
    q = q_ref[0, 0] * HEAD_DIM ** -0.5
    q8 = jnp.broadcast_to(q, (8, HEAD_DIM))
    scores = [lax.dot_general(q8, kr[0, 0, 0], _NT, precision=HIGHEST, preferred_element_type=F32) for kr in k_refs]
    s_new = jnp.sum(q8 * kn_ref[0, 0], axis=-1, keepdims=True)
    m = s_new
    for s in scores:
        m = jnp.maximum(m, jnp.max(s, axis=-1, keepdims=True))
    e_new = jnp.exp(s_new - m)
    l = e_new
    acc = e_new * vn_ref[0, 0]
    for s, vr in zip(scores, v_refs):
        e = jnp.exp(s - m)
        l = l + jnp.sum(e, axis=-1, keepdims=True)
        acc = acc + jnp.dot(e, vr[0, 0, 0], precision=HIGHEST, preferred_element_type=F32)
    o_ref[0, 0] = (acc / l)[0:1, :]


def _moba_sample_attn(page_table, idx, q, k_new, v_new, pool_k, pool_v):
    nbat, hh, _, d = q.shape
    n_sel = idx.shape[1]
    idx = idx.reshape(nbat, n_sel * hh)
    ps = pool_k.shape[3]
    ppb = MOBA_BLOCK // PAGE_SIZE

    def page_map(b, h, pt, ix, *, j, r):
        return (0, pt[b, ix[b, j * hh + h] * ppb + r], h, 0, 0)

    page_specs = [pl.BlockSpec((1, 1, 1, ps, d), functools.partial(page_map, j=j, r=r))
                  for j in range(n_sel) for r in range(ppb)]
    tok = pl.BlockSpec((1, 1, 1, d), lambda b, h, pt, ix: (b, h, 0, 0))
    gs = pltpu.PrefetchScalarGridSpec(
        num_scalar_prefetch=2, grid=(nbat, hh),
        in_specs=[tok, tok, tok] + page_specs + page_specs,
        out_specs=tok)
    n_pg = len(page_specs)
    return pl.pallas_call(
        _moba_sample_attn_kernel, grid_spec=gs,
        out_shape=jax.ShapeDtypeStruct((nbat, hh, 1, d), F32),
        compiler_params=_params("parallel", "arbitrary"),
        name="moba_sample_attn",
    )(page_table, idx, q, k_new, v_new, *([pool_k] * n_pg), *([pool_v] * n_pg))


def _cross_sample_kernel(q_ref, mk_ref, mv_ref, o_ref):
    q = q_ref[0]
    outs = []
    for hd in range(CROSS_HEADS):
        sl = slice(hd * HEAD_DIM, (hd + 1) * HEAD_DIM)
        q8 = jnp.broadcast_to(q[:, sl], (8, HEAD_DIM))
        s = lax.dot_general(q8, mk_ref[0, :, sl], _NT, precision=HIGHEST, preferred_element_type=F32) * HEAD_DIM ** -0.5
        e = jnp.exp(s - jnp.max(s, axis=-1, keepdims=True))
        p = e / jnp.sum(e, axis=-1, keepdims=True)
        outs.append(jnp.dot(p, mv_ref[0, :, sl], precision=HIGHEST, preferred_element_type=F32)[0:1, :])
    o_ref[0] = jnp.concatenate(outs, axis=-1)


def _cross_sample(q, mem_k, mem_v):
    nbat = q.shape[0]
    row = lambda a: pl.BlockSpec((1,) + a.shape[1:], lambda b: (b,) + (0,) * (a.ndim - 1))
    return pl.pallas_call(
        _cross_sample_kernel,
        grid=(nbat,),
        in_specs=[row(q), row(mem_k), row(mem_v)],
        out_specs=row(q),
        out_shape=jax.ShapeDtypeStruct(q.shape, F32),
        compiler_params=_params("parallel"),
        name="cross_sample",
    )(q, mem_k, mem_v)


def _rope_tables(pos):
    half = ROPE_DIM // 2
    inv_freq = ROPE_THETA ** (-jnp.arange(0, ROPE_DIM, 2, dtype=F32) / ROPE_DIM)
    ang = pos.astype(F32)[:, None] * inv_freq[None, :]
    cos, sin = jnp.cos(ang), jnp.sin(ang)
    n = pos.shape[0]
    cos_t = jnp.concatenate([cos, cos, jnp.ones((n, HEAD_DIM - ROPE_DIM), F32)], axis=-1)
    sin_t = jnp.concatenate([-sin, sin, jnp.zeros((n, HEAD_DIM - ROPE_DIM), F32)], axis=-1)
    return cos_t, sin_t


def kernel(x_prompt, x_sample, cache_moba_k, cache_moba_v, page_table, state_gdn, state_conv, cache_mem_k, cache_mem_v, mem_prompt, w_ffn1_norm, w_ffn1_gu, w_ffn1_down, w_mix_norm, w_in, gdn_conv_w, gdn_a_log, gdn_dt_bias, gdn_out_norm, w_out, w_cross_norm, w_cross_q, w_cross_out, w_mem_norm, w_mem_kv, w_ffn2_norm, w_ffn2_gu, w_ffn2_down, w_final_norm):
    assert x_prompt.shape[0] == 1 and x_sample.shape[1] == 1 and w_in.shape[0] == 1
    s_len, d_model = x_prompt.shape[1], x_prompt.shape[2]
    nbat = x_sample.shape[0]
    past_len = page_table.shape[1] * PAGE_SIZE
    assert past_len % MOBA_BLOCK == 0

    row = lambda v: v.reshape(1, -1).astype(F32)
    g_ffn1, g_mix, g_cross, g_mem, g_ffn2 = (row(w[0]) for w in (w_ffn1_norm, w_mix_norm, w_cross_norm, w_mem_norm, w_ffn2_norm))
    g_final = row(w_final_norm)
    gu1, dn1 = w_ffn1_gu[0].astype(BF16), w_ffn1_down[0].astype(BF16)
    gu2, dn2 = w_ffn2_gu[0].astype(BF16), w_ffn2_down[0].astype(BF16)
    gdn_end = 4 * GDN_WIDTH
    ba_end = gdn_end + 2 * GDN_HEADS
    w_main = jnp.concatenate([w_in[0][:, :gdn_end], w_in[0][:, ba_end:]], axis=1).astype(BF16)
    w_ba = jnp.pad(w_in[0][:, gdn_end:ba_end], ((0, 0), (0, LANES - 2 * GDN_HEADS))).astype(BF16)
    wo_gdn, wo_moba = w_out[0][:GDN_WIDTH].astype(BF16), w_out[0][GDN_WIDTH:].astype(BF16)
    wq_c, wo_c = w_cross_q[0].astype(BF16), w_cross_out[0].astype(BF16)
    w_kv = w_mem_kv[0].astype(BF16)
    conv_w = gdn_conv_w[0]
    lane_pad = lambda v: jnp.pad(v.reshape(1, -1).astype(F32), ((0, 0), (GDN_HEADS, LANES - 2 * GDN_HEADS)))
    alog_row, dtb_row = lane_pad(gdn_a_log[0]), lane_pad(gdn_dt_bias[0])
    onorm = row(gdn_out_norm[0])
    moba_col0 = gdn_end

    xp = x_prompt[0]
    tm = _tile(s_len, 512)
    h = _ffn(xp, g_ffn1, gu1, dn1, g_final, final_norm=False, tm=tm, tf=512)
    cols = _rms_matmul(h, g_mix, w_main, tm=_tile(s_len, 1024), tn=1024)
    ba = _rms_matmul(h, g_mix, w_ba, tm=_tile(s_len, 1024), tn=LANES)
    o_gdn, p_gdn_state, conv_tail = _gdn_prompt(cols, ba, conv_w, alog_row, dtb_row, onorm)
    cos_p, sin_p = _rope_tables(jnp.arange(s_len))
    q16, p_k, k16, p_v, vt16, bias = _moba_prep(cols, cos_p, sin_p, moba_col0)
    o_moba = _moba_attn(q16, k16, vt16, bias, tq=128)
    h = _proj_res(h, [(o_gdn, wo_gdn), (o_moba, wo_moba)], tm=tm)
    mem_kv = _rms_matmul(mem_prompt[0], g_mem, w_kv, tm=mem_prompt.shape[1], tn=2 * CROSS_WIDTH)
    mem_k, mem_v = mem_kv[:, :CROSS_WIDTH], mem_kv[:, CROSS_WIDTH:]
    h = _cross_prompt(h, g_cross, wq_c, mem_k.astype(BF16), mem_v.astype(BF16), wo_c, tm=tm)
    y_prompt = _ffn(h, g_ffn2, gu2, dn2, g_final, final_norm=True, tm=tm, tf=512)

    xs = x_sample[:, 0]
    hs = _ffn(xs, g_ffn1, gu1, dn1, g_final, final_norm=False, tm=nbat, tf=512)
    cols_s = _rms_matmul(hs, g_mix, w_main, tm=nbat, tn=1024)
    ba_s = _rms_matmul(hs, g_mix, w_ba, tm=nbat, tn=LANES)
    o_gdn_s, s_gdn_state, s_conv = _gdn_sample(
        cols_s[:, None, :GDN_CONV_CH], state_conv[0], cols_s[:, None, GDN_CONV_CH:gdn_end], ba_s[:, None, :],
        conv_w, alog_row, dtb_row, onorm, state_gdn[0])
    mq, mk, mv = (cols_s[:, moba_col0 + j * MOBA_WIDTH:moba_col0 + (j + 1) * MOBA_WIDTH].reshape(nbat, MOBA_HEADS, HEAD_DIM) for j in range(3))
    kmean = _moba_kmean(page_table, cache_moba_k)
    cos_s, sin_s = _rope_tables(jnp.full((1,), past_len))
    n_sel = min(MOBA_TOPK, past_len // MOBA_BLOCK)
    q_r, k_r, idx = _moba_sample_gate(mq, mk, cos_s, sin_s, kmean, n_sel)
    o_moba_s = _moba_sample_attn(page_table, idx, q_r[:, :, None, :], k_r[:, :, None, :], mv[:, :, None, :],
                                 cache_moba_k, cache_moba_v)
    hs = _proj_res(hs, [(o_gdn_s[:, 0], wo_gdn), (o_moba_s.reshape(nbat, MOBA_WIDTH), wo_moba)], tm=nbat)
    q_c = _rms_matmul(hs, g_cross, wq_c, tm=nbat, tn=CROSS_WIDTH)
    mem_tokens = cache_mem_k.shape[2]
    o_c = _cross_sample(q_c[:, None, :], cache_mem_k[0].reshape(nbat, mem_tokens, CROSS_WIDTH),
                        cache_mem_v[0].reshape(nbat, mem_tokens, CROSS_WIDTH))
    hs = _proj_res(hs, [(o_c[:, 0], wo_c)], tm=nbat)
    y_sample = _ffn(hs, g_ffn2, gu2, dn2, g_final, final_norm=True, tm=nbat, tf=512)

    return (y_prompt[None], y_sample[:, None, :],
            p_k[None, None], p_v[None, None],
            p_gdn_state[None, None], conv_tail[8 - (CONV_WIDTH - 1):][None, None],
            mem_k.reshape(1, 1, -1, CROSS_HEADS, HEAD_DIM), mem_v.reshape(1, 1, -1, CROSS_HEADS, HEAD_DIM),
            k_r[None, :, :, None, :], mv[None, :, :, None, :],
            s_gdn_state[None], s_conv[None])
```

```python
import functools

import jax
import jax.numpy as jnp
from jax import lax
from jax.experimental import pallas as pl
from jax.experimental.pallas import tpu as pltpu

F32 = jnp.float32
BF16 = jnp.bfloat16
HIGHEST = lax.Precision.HIGHEST

HEAD_DIM = 128
GDN_HEADS = 8
MOBA_HEADS = 8
GDN_WIDTH = GDN_HEADS * HEAD_DIM
MOBA_WIDTH = MOBA_HEADS * HEAD_DIM
CONV_WIDTH = 4
GDN_CONV_CH = 3 * GDN_WIDTH
GDN_CHUNK = 64
MOBA_BLOCK = 256
MOBA_TOPK = 3
PAGE_SIZE = 128
ROPE_DIM = HEAD_DIM // 4
ROPE_THETA = 500000.0
CROSS_HEADS = 4
CROSS_WIDTH = CROSS_HEADS * HEAD_DIM
NORM_EPS = 1e-6

LANES = 128
MASK_VALUE = -1e30
LOG2_E = 1.4426950408889634
VT_ROWS = HEAD_DIM + 16
VMEM_LIMIT_BYTES = 56 * 1024 * 1024

_NT = (((1,), (1,)), ((), ()))
_TN = (((0,), (0,)), ((), ()))


def _params(*sem):
    return pltpu.CompilerParams(dimension_semantics=sem, vmem_limit_bytes=VMEM_LIMIT_BYTES)


def _rms(x, g):
    return x * lax.rsqrt(jnp.mean(x * x, axis=-1, keepdims=True) + NORM_EPS) * g


def _silu(x):
    return x * jax.nn.sigmoid(x)


def _dot(a, b):
    return jnp.dot(a, b, preferred_element_type=F32)


def _dot_nt(a, b):
    return lax.dot_general(a, b, _NT, preferred_element_type=F32)


def _dot_f32(a, b):
    return jnp.dot(a, b, precision=HIGHEST, preferred_element_type=F32)


def _col(x, c, width):
    return jnp.broadcast_to(x[:, c:c + 1], (x.shape[0], width))


def _tile(m, pref):
    return pref if m % pref == 0 else m


def _rms_matmul_kernel(x_ref, g_ref, w_ref, o_ref, xn_ref):
    @pl.when(pl.program_id(1) == 0)
    def _():
        xn_ref[...] = _rms(x_ref[...], g_ref[...]).astype(BF16)

    o_ref[...] = _dot(xn_ref[...], w_ref[...])


def _rms_matmul(x, g, w, *, tm, tn):
    m, d = x.shape
    n = w.shape[1]
    return pl.pallas_call(
        _rms_matmul_kernel,
        grid=(m // tm, n // tn),
        in_specs=[pl.BlockSpec((tm, d), lambda i, j: (i, 0)),
                  pl.BlockSpec((1, d), lambda i, j: (0, 0)),
                  pl.BlockSpec((d, tn), lambda i, j: (0, j))],
        out_specs=pl.BlockSpec((tm, tn), lambda i, j: (i, j)),
        out_shape=jax.ShapeDtypeStruct((m, n), F32),
        scratch_shapes=[pltpu.VMEM((tm, d), BF16)],
        compiler_params=_params("parallel", "arbitrary"),
        name="rms_matmul",
    )(x, g, w)


def _ffn_kernel(x_ref, g_ref, wg_ref, wu_ref, wd_ref, fg_ref, o_ref, xn_ref, *, final_norm):
    j = pl.program_id(1)

    @pl.when(j == 0)
    def _():
        xn_ref[...] = _rms(x_ref[...], g_ref[...]).astype(BF16)
        o_ref[...] = jnp.zeros_like(o_ref)

    xn = xn_ref[...]
    act = _silu(_dot(xn, wg_ref[...])) * _dot(xn, wu_ref[...])
    o_ref[...] += _dot(act.astype(BF16), wd_ref[...])

    @pl.when(j == pl.num_programs(1) - 1)
    def _():
        h = x_ref[...] + 0.5 * o_ref[...]
        if final_norm:
            h = _rms(h, fg_ref[...])
        o_ref[...] = h


def _ffn(x, g, w_gu, w_down, fg, *, final_norm, tm, tf):
    m, d = x.shape
    f = w_down.shape[0]
    nf = f // tf
    return pl.pallas_call(
        functools.partial(_ffn_kernel, final_norm=final_norm),
        grid=(m // tm, nf),
        in_specs=[pl.BlockSpec((tm, d), lambda i, j: (i, 0)),
                  pl.BlockSpec((1, d), lambda i, j: (0, 0)),
                  pl.BlockSpec((d, tf), lambda i, j: (0, j)),
                  pl.BlockSpec((d, tf), lambda i, j: (0, nf + j)),
                  pl.BlockSpec((tf, d), lambda i, j: (j, 0)),
                  pl.BlockSpec((1, d), lambda i, j: (0, 0))],
        out_specs=pl.BlockSpec((tm, d), lambda i, j: (i, 0)),
        out_shape=jax.ShapeDtypeStruct((m, d), F32),
        scratch_shapes=[pltpu.VMEM((tm, d), BF16)],
        compiler_params=_params("parallel", "arbitrary"),
        name="ffn",
    )(x, g, w_gu, w_gu, w_down, fg)


def _proj_res_kernel(*refs):
    res_ref, o_ref = refs[0], refs[-1]
    pairs = refs[1:-1]
    acc = res_ref[...]
    for a_ref, w_ref in zip(pairs[0::2], pairs[1::2]):
        acc = acc + _dot(a_ref[...].astype(BF16), w_ref[...])
    o_ref[...] = acc


def _proj_res(res, pairs, *, tm):
    m, d = res.shape
    in_specs = [pl.BlockSpec((tm, d), lambda i: (i, 0))]
    args = [res]
    for a, w in pairs:
        in_specs.append(pl.BlockSpec((tm, a.shape[1]), lambda i: (i, 0)))
        in_specs.append(pl.BlockSpec(w.shape, lambda i: (0, 0)))
        args += [a, w]
    return pl.pallas_call(
        _proj_res_kernel,
        grid=(m // tm,),
        in_specs=in_specs,
        out_specs=pl.BlockSpec((tm, d), lambda i: (i, 0)),
        out_shape=jax.ShapeDtypeStruct((m, d), F32),
        compiler_params=_params("parallel"),
        name="proj_res",
    )(*args)


def _cross_kernel(h_ref, g_ref, wq_ref, mk_ref, mv_ref, wo_ref, o_ref):
    h = h_ref[...]
    q = _dot(_rms(h, g_ref[...]).astype(BF16), wq_ref[...])
    outs = []
    for hd in range(CROSS_HEADS):
        sl = slice(hd * HEAD_DIM, (hd + 1) * HEAD_DIM)
        s = _dot_nt(q[:, sl].astype(BF16), mk_ref[:, sl]) * HEAD_DIM ** -0.5
        e = jnp.exp(s - jnp.max(s, axis=-1, keepdims=True))
        p = e / jnp.sum(e, axis=-1, keepdims=True)
        outs.append(_dot(p.astype(BF16), mv_ref[:, sl]).astype(BF16))
    o_ref[...] = h + _dot(jnp.concatenate(outs, axis=-1), wo_ref[...])


def _cross_prompt(h, g, wq, mem_k, mem_v, wo, *, tm):
    m, d = h.shape
    full = lambda a: pl.BlockSpec(a.shape, lambda i: (0, 0))
    return pl.pallas_call(
        _cross_kernel,
        grid=(m // tm,),
        in_specs=[pl.BlockSpec((tm, d), lambda i: (i, 0)), full(g), full(wq), full(mem_k), full(mem_v), full(wo)],
        out_specs=pl.BlockSpec((tm, d), lambda i: (i, 0)),
        out_shape=jax.ShapeDtypeStruct((m, d), F32),
        compiler_params=_params("parallel"),
        name="cross_prompt",
    )(h, g, wq, mem_k, mem_v, wo)


def _split(x):
    hi = x.astype(BF16)
    return hi, (x - hi.astype(F32)).astype(BF16)


def _dot_split(a, b):
    return _dot(a[0], b[0]) + _dot(a[0], b[1]) + _dot(a[1], b[0])


def _gdn_prompt_kernel(qkv_ref, z_ref, ba_ref, cw_ref, alog_ref, dtb_ref, onorm_ref,
                       o_ref, sfin_ref, convn_ref, xp_ref, y_ref, s_ref):
    c = GDN_CHUNK
    rows = qkv_ref.shape[0]
    nck = rows // c
    i = pl.program_id(0)

    @pl.when(i == 0)
    def _():
        xp_ref[0:8, :] = jnp.zeros((8, GDN_CONV_CH), F32)
        s_ref[...] = jnp.zeros_like(s_ref)

    xp_ref[8:8 + rows, :] = qkv_ref[...]
    y = cw_ref[0:1, :] * xp_ref[5:5 + rows, :]
    for j in range(1, CONV_WIDTH):
        y = y + cw_ref[j:j + 1, :] * xp_ref[5 + j:5 + j + rows, :]
    y_ref[...] = _silu(y)
    tail = xp_ref[rows:rows + 8, :]
    xp_ref[0:8, :] = tail
    convn_ref[...] = tail

    ba = ba_ref[...]
    beta = jax.nn.sigmoid(ba)
    g = -jnp.exp(alog_ref[...]) * jax.nn.softplus(ba + dtb_ref[...])
    row = lax.broadcasted_iota(jnp.int32, (c, c), 0)
    colv = lax.broadcasted_iota(jnp.int32, (c, c), 1)
    causal = row >= colv
    strict = row > colv
    eye = (row == colv).astype(F32)
    tri = causal.astype(F32)
    heads = range(GDN_HEADS)

    work = []
    for ck in range(nck):
        r0 = ck * c
        gc = _dot_f32(tri, g[r0:r0 + c])
        gc_t = gc.T
        eg = jnp.exp(gc)
        g_last = gc[c - 1:c, :]
        eg_last = jnp.exp(g_last)
        k_decay = jnp.exp(g_last - gc)
        for h in heads:
            gh = GDN_HEADS + h
            qh = y_ref[r0:r0 + c, h * HEAD_DIM:(h + 1) * HEAD_DIM]
            kh = y_ref[r0:r0 + c, GDN_WIDTH + h * HEAD_DIM:GDN_WIDTH + (h + 1) * HEAD_DIM]
            vh = y_ref[r0:r0 + c, 2 * GDN_WIDTH + h * HEAD_DIM:2 * GDN_WIDTH + (h + 1) * HEAD_DIM]
            qn = qh * lax.rsqrt(jnp.sum(qh * qh, axis=-1, keepdims=True) + NORM_EPS) * HEAD_DIM ** -0.5
            kn = kh * lax.rsqrt(jnp.sum(kh * kh, axis=-1, keepdims=True) + NORM_EPS)
            b_col = _col(beta[r0:r0 + c], h, HEAD_DIM)
            eg_col = _col(eg, gh, HEAD_DIM)
            diff = _col(gc, gh, c) - jnp.broadcast_to(gc_t[gh:gh + 1, :], (c, c))
            decay = jnp.exp(jnp.where(causal, diff, -jnp.inf))
            kb = kn * b_col
            kn16 = kn.astype(BF16)
            qn16 = qn.astype(BF16)
            work.append(dict(
                a=_dot_nt(kb.astype(BF16), kn16) * jnp.where(strict, decay, 0.0),
                attn=(_dot_nt(qn16, kn16) * decay).astype(BF16),
                vb=(vh * b_col).astype(BF16),
                kbg=(kb * eg_col).astype(BF16),
                qg=(qn * eg_col).astype(BF16),
                kd=(kn * _col(k_decay, gh, HEAD_DIM)).astype(BF16),
                eg_last=eg_last[:, gh:gh + 1]))

    ps = [_split(-wk["a"]) for wk in work]
    ts = [eye - wk["a"] for wk in work]
    for _ in range(5):
        ps = [_split(_dot_split(p, p)) for p in ps]
        ts = [t + _dot_split(_split(t), p) for t, p in zip(ts, ps)]
    for wk, t in zip(work, ts):
        t16 = t.astype(BF16)
        wk["u"] = _dot(t16, wk["vb"])
        wk["w"] = _dot(t16, wk["kbg"]).astype(BF16)

    for ck in range(nck):
        r0 = ck * c
        wks = work[ck * GDN_HEADS:(ck + 1) * GDN_HEADS]
        states = [s_ref[h] for h in heads]
        s16 = [s.astype(BF16) for s in states]
        v_new = [(wk["u"] - _dot(wk["w"], s16[h])).astype(BF16) for h, wk in zip(heads, wks)]
        outs = [_dot(wk["qg"], s16[h]) + _dot(wk["attn"], v_new[h]) for h, wk in zip(heads, wks)]
        for h, wk in zip(heads, wks):
            s_ref[h] = states[h] * wk["eg_last"] + lax.dot_general(wk["kd"], v_new[h], _TN, preferred_element_type=F32)
        for h, o in zip(heads, outs):
            sl = slice(h * HEAD_DIM, (h + 1) * HEAD_DIM)
            o = o * lax.rsqrt(jnp.mean(o * o, axis=-1, keepdims=True) + NORM_EPS) * onorm_ref[...]
            o_ref[r0:r0 + c, sl] = o * _silu(z_ref[r0:r0 + c, sl])

    @pl.when(i == pl.num_programs(0) - 1)
    def _():
        sfin_ref[...] = s_ref[...]


def _gdn_prompt(cols, ba, conv_w, alog_row, dtb_row, onorm, *, chunks_per_step):
    s_len = cols.shape[0]
    c = GDN_CHUNK * chunks_per_step
    assert s_len % c == 0
    full = lambda a: pl.BlockSpec(a.shape, lambda i: (0,) * a.ndim)
    return pl.pallas_call(
        _gdn_prompt_kernel,
        grid=(s_len // c,),
        in_specs=[pl.BlockSpec((c, GDN_CONV_CH), lambda i: (i, 0)),
                  pl.BlockSpec((c, GDN_WIDTH), lambda i: (i, GDN_CONV_CH // GDN_WIDTH)),
                  pl.BlockSpec((c, LANES), lambda i: (i, 0)),
                  full(conv_w), full(alog_row), full(dtb_row), full(onorm)],
        out_specs=[pl.BlockSpec((c, GDN_WIDTH), lambda i: (i, 0)),
                   pl.BlockSpec((GDN_HEADS, HEAD_DIM, HEAD_DIM), lambda i: (0, 0, 0)),
                   pl.BlockSpec((8, GDN_CONV_CH), lambda i: (0, 0))],
        out_shape=[jax.ShapeDtypeStruct((s_len, GDN_WIDTH), F32),
                   jax.ShapeDtypeStruct((GDN_HEADS, HEAD_DIM, HEAD_DIM), F32),
                   jax.ShapeDtypeStruct((8, GDN_CONV_CH), F32)],
        scratch_shapes=[pltpu.VMEM((c + 8, GDN_CONV_CH), F32),
                        pltpu.VMEM((c, GDN_CONV_CH), F32),
                        pltpu.VMEM((GDN_HEADS, HEAD_DIM, HEAD_DIM), F32)],
        compiler_params=_params("arbitrary"),
        name="gdn_prompt",
    )(cols, cols, ba, conv_w, alog_row, dtb_row, onorm)


def _gdn_sample_kernel(x_ref, cs_ref, z_ref, ba_ref, cw_ref, alog_ref, dtb_ref, onorm_ref, s0_ref,
                       o_ref, s_ref, convn_ref):
    x = x_ref[0]
    cs = cs_ref[0]
    y = cw_ref[CONV_WIDTH - 1:CONV_WIDTH, :] * x
    for j in range(CONV_WIDTH - 1):
        y = y + cw_ref[j:j + 1, :] * cs[j:j + 1, :]
    y = _silu(y)
    convn_ref[0] = jnp.concatenate([cs[1:CONV_WIDTH - 1, :], x], axis=0)

    ba = ba_ref[0]
    beta = jax.nn.sigmoid(ba)
    eg = jnp.exp(-jnp.exp(alog_ref[...]) * jax.nn.softplus(ba + dtb_ref[...]))
    z = z_ref[0]

    def head_rows(base):
        return jnp.concatenate([y[:, base + h * HEAD_DIM:base + (h + 1) * HEAD_DIM] for h in range(GDN_HEADS)], axis=0)

    q8 = head_rows(0)
    k8 = head_rows(GDN_WIDTH)
    v8 = head_rows(2 * GDN_WIDTH)
    q8 = q8 * lax.rsqrt(jnp.sum(q8 * q8, axis=-1, keepdims=True) + NORM_EPS) * HEAD_DIM ** -0.5
    k8 = k8 * lax.rsqrt(jnp.sum(k8 * k8, axis=-1, keepdims=True) + NORM_EPS)
    qk8 = jnp.sum(q8 * k8, axis=-1, keepdims=True)
    q_t = q8.T
    k_t = k8.T
    outs = []
    for h in range(GDN_HEADS):
        s = s0_ref[0, h]
        k_col = _col(k_t, h, HEAD_DIM)
        q_col = _col(q_t, h, HEAD_DIM)
        e = eg[:, GDN_HEADS + h:GDN_HEADS + h + 1]
        k_s = jnp.sum(k_col * s, axis=0, keepdims=True)
        q_s = jnp.sum(q_col * s, axis=0, keepdims=True)
        v_new = beta[:, h:h + 1] * (v8[h:h + 1, :] - e * k_s)
        o = e * q_s + qk8[h:h + 1, :] * v_new
        s_ref[0, h] = s * e + k_col * v_new
        o = o * lax.rsqrt(jnp.mean(o * o, axis=-1, keepdims=True) + NORM_EPS) * onorm_ref[...]
        outs.append(o * _silu(z[:, h * HEAD_DIM:(h + 1) * HEAD_DIM]))
    o_ref[0] = jnp.concatenate(outs, axis=-1)


def _gdn_sample(x, conv_state, z, ba, conv_w, alog_row, dtb_row, onorm, s0):
    nb = x.shape[0]
    full = lambda a: pl.BlockSpec(a.shape, lambda b: (0,) * a.ndim)
    row = lambda a: pl.BlockSpec((1,) + a.shape[1:], lambda b: (b,) + (0,) * (a.ndim - 1))
    return pl.pallas_call(
        _gdn_sample_kernel,
        grid=(nb,),
        in_specs=[row(x), row(conv_state), row(z), row(ba), full(conv_w), full(alog_row), full(dtb_row),
                  full(onorm), row(s0)],
        out_specs=[pl.BlockSpec((1, 1, GDN_WIDTH), lambda b: (b, 0, 0)), row(s0), row(conv_state)],
        out_shape=[jax.ShapeDtypeStruct((nb, 1, GDN_WIDTH), F32),
                   jax.ShapeDtypeStruct(s0.shape, F32),
                   jax.ShapeDtypeStruct(conv_state.shape, F32)],
        compiler_params=_params("parallel"),
        name="gdn_sample",
    )(x, conv_state, z, ba, conv_w, alog_row, dtb_row, onorm, s0)


def _rope(x, cos, sin_signed):
    lane = lax.broadcasted_iota(jnp.int32, x.shape, x.ndim - 1)
    half = ROPE_DIM // 2
    partner = jnp.where(lane < half, pltpu.roll(x, LANES - half, x.ndim - 1), pltpu.roll(x, half, x.ndim - 1))
    return x * cos + partner * sin_signed


def _moba_prep_kernel(q_ref, k_ref, v_ref, cos_ref, sin_ref,
                      q16_ref, kf_ref, k16_ref, vf_ref, vt16_ref, bias_ref, kmean_ref, *, n_sel):
    i = pl.program_id(0)
    nb = kmean_ref.shape[1]
    t = q_ref.shape[0]

    @pl.when(i == 0)
    def _():
        kmean_ref[...] = jnp.zeros_like(kmean_ref)

    cos = cos_ref[...]
    sin = sin_ref[...]
    blk = lax.broadcasted_iota(jnp.int32, (nb, t), 0)
    past = blk < i
    for h in range(MOBA_HEADS):
        sl = slice(h * HEAD_DIM, (h + 1) * HEAD_DIM)
        q = _rope(q_ref[:, sl], cos, sin)
        k = _rope(k_ref[:, sl], cos, sin)
        v = v_ref[:, sl]
        q16_ref[h] = (q * (HEAD_DIM ** -0.5 * LOG2_E)).astype(BF16)
        kf_ref[h] = k
        k16_ref[h] = k.astype(BF16)
        vf_ref[h] = v
        vt16_ref[h, 0, 0:HEAD_DIM, :] = v.T.astype(BF16)
        vt16_ref[h, 0, HEAD_DIM:VT_ROWS, :] = (lax.broadcasted_iota(jnp.int32, (VT_ROWS - HEAD_DIM, t), 0) == 0).astype(BF16)
        gate = lax.dot_general(kmean_ref[h], q, _NT, precision=HIGHEST, preferred_element_type=F32)
        gate = jnp.where(past, gate, -jnp.inf)
        chosen = jnp.zeros((nb, t), F32)
        for _ in range(n_sel):
            top = jnp.max(gate, axis=0, keepdims=True)
            first = jnp.min(jnp.where(gate == top, blk, nb), axis=0, keepdims=True)
            pick = (blk == first) & (top > -jnp.inf)
            chosen = jnp.where(pick, 1.0, chosen)
            gate = jnp.where(pick, -jnp.inf, gate)
        bias_ref[h] = jnp.where(chosen > 0.0, 0.0, MASK_VALUE)
        kmean_ref[h, pl.ds(i, 1), :] = jnp.mean(k, axis=0, keepdims=True)


def _moba_prep(cols, cos, sin, col0):
    s_len = cols.shape[0]
    t = MOBA_BLOCK
    assert s_len % t == 0
    nb = s_len // t
    hh = MOBA_HEADS
    cb = col0 // MOBA_WIDTH
    n_sel = min(MOBA_TOPK, nb - 1)
    return pl.pallas_call(
        functools.partial(_moba_prep_kernel, n_sel=n_sel),
        grid=(nb,),
        in_specs=[pl.BlockSpec((t, MOBA_WIDTH), lambda i: (i, cb)),
                  pl.BlockSpec((t, MOBA_WIDTH), lambda i: (i, cb + 1)),
                  pl.BlockSpec((t, MOBA_WIDTH), lambda i: (i, cb + 2)),
                  pl.BlockSpec((t, LANES), lambda i: (i, 0)),
                  pl.BlockSpec((t, LANES), lambda i: (i, 0))],
        out_specs=[pl.BlockSpec((hh, t, HEAD_DIM), lambda i: (0, i, 0)),
                   pl.BlockSpec((hh, t, HEAD_DIM), lambda i: (0, i, 0)),
                   pl.BlockSpec((hh, t, HEAD_DIM), lambda i: (0, i, 0)),
                   pl.BlockSpec((hh, t, HEAD_DIM), lambda i: (0, i, 0)),
                   pl.BlockSpec((hh, 1, VT_ROWS, t), lambda i: (0, i, 0, 0)),
                   pl.BlockSpec((hh, nb, t), lambda i: (0, 0, i))],
        out_shape=[jax.ShapeDtypeStruct((hh, s_len, HEAD_DIM), BF16),
                   jax.ShapeDtypeStruct((hh, s_len, HEAD_DIM), F32),
                   jax.ShapeDtypeStruct((hh, s_len, HEAD_DIM), BF16),
                   jax.ShapeDtypeStruct((hh, s_len, HEAD_DIM), F32),
                   jax.ShapeDtypeStruct((hh, nb, VT_ROWS, t), BF16),
                   jax.ShapeDtypeStruct((hh, nb, s_len), F32)],
        scratch_shapes=[pltpu.VMEM((hh, nb, HEAD_DIM), F32)],
        compiler_params=_params("arbitrary"),
        name="moba_prep",
    )(cols, cols, cols, cos, sin)


_KV_PAIR = 2


def _moba_attn_kernel(q_ref, k_ref, vt_ref, bias_ref, o_ref, acc_ref, s_ref, p_ref):
    hg, tq, _ = q_ref.shape
    tk = vt_ref.shape[3]
    own = pl.program_id(1)
    diag = own // _KV_PAIR
    kpos = lax.broadcasted_iota(jnp.int32, (tk, tq), 0)
    qpos = lax.broadcasted_iota(jnp.int32, (tk, tq), 1)
    causal_bias = jnp.where(kpos <= qpos, 0.0, MASK_VALUE)

    def scores(j, slot, diagonal=False):
        g = diag if diagonal else jnp.minimum(j - 1, diag)
        real = j <= diag
        off = pl.multiple_of(g * (_KV_PAIR * tk), _KV_PAIR * tk)
        for h in range(hg):
            s = _dot_nt(k_ref[h, pl.ds(off, _KV_PAIR * tk), :], q_ref[h])
            for r in range(_KV_PAIR):
                kb = g * _KV_PAIR + r
                if diagonal:
                    row = jnp.broadcast_to(bias_ref[h, pl.ds(kb, 1), :], (tk, tq))
                    bias = jnp.where(kb == own, causal_bias, row)
                else:
                    bias = jnp.where(real, bias_ref[h, pl.ds(kb, 1), :], MASK_VALUE)
                s_ref[slot, h, r * tk:(r + 1) * tk, :] = s[r * tk:(r + 1) * tk] + bias

    def softmax(slot, ms):
        new, alphas = [], []
        for h in range(hg):
            s = s_ref[slot, h]
            m_new = jnp.maximum(ms[h], jnp.max(s, axis=0, keepdims=True))
            p_ref[slot, h] = jnp.exp2(s - m_new).astype(BF16)
            new.append(m_new)
            alphas.append(jnp.exp2(ms[h] - m_new))
        return tuple(new), tuple(alphas)

    def accumulate(j, slot, alphas):
        g = jnp.where(j == 0, diag, jnp.minimum(j - 1, diag))
        for h in range(hg):
            pv = _dot(vt_ref[h, g * _KV_PAIR], p_ref[slot, h, 0:tk, :])
            for r in range(1, _KV_PAIR):
                pv = pv + _dot(vt_ref[h, g * _KV_PAIR + r], p_ref[slot, h, r * tk:(r + 1) * tk, :])
            acc_ref[h] = acc_ref[h] * alphas[h] + pv

    acc_ref[...] = jnp.zeros_like(acc_ref)
    m0 = jnp.full((1, tq), MASK_VALUE, F32)
    scores(0, 0, diagonal=True)
    scores(1, 1)
    carry = softmax(0, (m0,) * hg)

    def step(j, cur, carry):
        ms, alphas = carry
        scores(j + 1, 1 - cur)
        new = softmax(cur, ms)
        accumulate(j - 1, 1 - cur, alphas)
        return new

    def two_steps(t, carry):
        return step(2 * t + 2, 0, step(2 * t + 1, 1, carry))

    half = (diag + 1) // 2
    _, alphas = lax.fori_loop(0, half, two_steps, carry)
    accumulate(2 * half, 0, alphas)
    for h in range(hg):
        acc = acc_ref[h]
        o_ref[:, h * HEAD_DIM:(h + 1) * HEAD_DIM] = (acc[0:HEAD_DIM] / acc[HEAD_DIM:HEAD_DIM + 1]).T


def _moba_attn(q16, k16, vt16, bias, *, hg):
    hh, s_len, d = q16.shape
    nb, rows, tk = vt16.shape[1:]
    assert nb % _KV_PAIR == 0 and hh % hg == 0
    once = pl.Buffered(1)
    return pl.pallas_call(
        _moba_attn_kernel,
        grid=(hh // hg, nb),
        in_specs=[pl.BlockSpec((hg, tk, d), lambda h, i: (h, i, 0)),
                  pl.BlockSpec((hg, s_len, d), lambda h, i: (h, 0, 0), pipeline_mode=once),
                  pl.BlockSpec((hg, nb, rows, tk), lambda h, i: (h, 0, 0, 0), pipeline_mode=once),
                  pl.BlockSpec((hg, nb, tk), lambda h, i: (h, 0, i))],
        out_specs=pl.BlockSpec((tk, hg * d), lambda h, i: (i, h)),
        out_shape=jax.ShapeDtypeStruct((s_len, hh * d), F32),
        scratch_shapes=[pltpu.VMEM((hg, rows, tk), F32),
                        pltpu.VMEM((2, hg, _KV_PAIR * tk, tk), F32),
                        pltpu.VMEM((2, hg, _KV_PAIR * tk, tk), BF16)],
        compiler_params=_params("parallel", "arbitrary"),
        name="moba_attn",
    )(q16, k16, vt16, bias)


_PAGES_PER_STEP = 8


def _moba_kmean_kernel(pt_ref, *refs):
    del pt_ref
    o_ref = refs[-1]
    ppb = MOBA_BLOCK // PAGE_SIZE
    for blk in range(_PAGES_PER_STEP // ppb):
        tot = jnp.sum(refs[blk * ppb][0, 0], axis=1)
        for r in range(1, ppb):
            tot = tot + jnp.sum(refs[blk * ppb + r][0, 0], axis=1)
        o_ref[0, blk] = tot * (1.0 / MOBA_BLOCK)


def _moba_kmean(page_table, pool_k):
    nbat, n_pages = page_table.shape
    _, _, hh, ps, d = pool_k.shape
    ppb = MOBA_BLOCK // PAGE_SIZE
    assert n_pages % _PAGES_PER_STEP == 0
    steps = n_pages // _PAGES_PER_STEP
    in_specs = [pl.BlockSpec((1, 1, hh, ps, d), functools.partial(lambda b, c, pt, r: (0, pt[b, c * _PAGES_PER_STEP + r], 0, 0, 0), r=r))
                for r in range(_PAGES_PER_STEP)]
    gs = pltpu.PrefetchScalarGridSpec(
        num_scalar_prefetch=1, grid=(nbat, steps), in_specs=in_specs,
        out_specs=pl.BlockSpec((1, _PAGES_PER_STEP // ppb, hh, d), lambda b, c, pt: (b, c, 0, 0)))
    return pl.pallas_call(
        _moba_kmean_kernel, grid_spec=gs,
        out_shape=jax.ShapeDtypeStruct((nbat, n_pages // ppb, hh, d), F32),
        compiler_params=_params("parallel", "arbitrary"),
        name="moba_kmean",
    )(page_table, *([pool_k] * _PAGES_PER_STEP))


def _moba_sample_gate_kernel(q_ref, k_ref, cos_ref, sin_ref, km_ref, qr_ref, kr_ref, idx_ref, *, n_sel):
    q = _rope(q_ref[0], cos_ref[...], sin_ref[...])
    kr_ref[0] = _rope(k_ref[0], cos_ref[...], sin_ref[...])
    qr_ref[0] = q
    gate = jnp.sum(km_ref[0] * q, axis=-1)
    nb = gate.shape[0]
    blk = lax.broadcasted_iota(jnp.int32, gate.shape, 0)
    rows = []
    for _ in range(n_sel):
        top = jnp.max(gate, axis=0, keepdims=True)
        first = jnp.min(jnp.where(gate == top, blk, nb), axis=0, keepdims=True)
        rows.append(first)
        gate = jnp.where(blk == first, -jnp.inf, gate)
    idx_ref[0] = jnp.concatenate(rows, axis=0)


def _moba_sample_gate(q, k, cos, sin, kmean, n_sel):
    nbat, hh, d = q.shape
    nb = kmean.shape[1]
    row = lambda a: pl.BlockSpec((1,) + a.shape[1:], lambda b: (b,) + (0,) * (a.ndim - 1))
    full = lambda a: pl.BlockSpec(a.shape, lambda b: (0,) * a.ndim)
    return pl.pallas_call(
        functools.partial(_moba_sample_gate_kernel, n_sel=n_sel),
        grid=(nbat,),
        in_specs=[row(q), row(k), full(cos), full(sin), row(kmean)],
        out_specs=[row(q), row(k), pl.BlockSpec((1, n_sel, hh), lambda b: (b, 0, 0))],
        out_shape=[jax.ShapeDtypeStruct(q.shape, F32), jax.ShapeDtypeStruct(k.shape, F32),
                   jax.ShapeDtypeStruct((nbat, n_sel, hh), jnp.int32)],
        compiler_params=_params("parallel"),
        name="moba_sample_gate",
    )(q, k, cos, sin, kmean)


def _moba_sample_attn_kernel(pt_ref, idx_ref, q_ref, kn_ref, vn_ref, *refs):
    del pt_ref, idx_ref
    o_ref = refs[-1]
    n = (len(refs) - 1) // 2
    k_refs, v_refs = refs[:n], refs[n:2 * n]
    q = q_ref[0, 0] * HEAD_DIM ** -0.5
    q8 = jnp.broadcast_to(q, (8, HEAD_DIM))
    q16 = q8.astype(BF16)
    scores = [_dot_nt(q16, kr[0, 0, 0].astype(BF16)) for kr in k_refs]
    s_new = jnp.sum(q8 * kn_ref[0, 0], axis=-1, keepdims=True)
    m = s_new
    for s in scores:
        m = jnp.maximum(m, jnp.max(s, axis=-1, keepdims=True))
    e_new = jnp.exp(s_new - m)
    l = e_new
    acc = e_new * vn_ref[0, 0]
    for s, vr in zip(scores, v_refs):
        e = jnp.exp(s - m)
        l = l + jnp.sum(e, axis=-1, keepdims=True)
        acc = acc + _dot(e.astype(BF16), vr[0, 0, 0].astype(BF16))
    o_ref[0, 0] = (acc / l)[0:1, :]


def _moba_sample_attn(page_table, idx, q, k_new, v_new, pool_k, pool_v):
    nbat, hh, _, d = q.shape
    n_sel = idx.shape[1]
    idx = idx.reshape(nbat, n_sel * hh)
    ps = pool_k.shape[3]
    ppb = MOBA_BLOCK // PAGE_SIZE

    def page_map(b, h, pt, ix, *, j, r):
        return (0, pt[b, ix[b, j * hh + h] * ppb + r], h, 0, 0)

    page_specs = [pl.BlockSpec((1, 1, 1, ps, d), functools.partial(page_map, j=j, r=r))
                  for j in range(n_sel) for r in range(ppb)]
    tok = pl.BlockSpec((1, 1, 1, d), lambda b, h, pt, ix: (b, h, 0, 0))
    gs = pltpu.PrefetchScalarGridSpec(
        num_scalar_prefetch=2, grid=(nbat, hh),
        in_specs=[tok, tok, tok] + page_specs + page_specs,
        out_specs=tok)
    n_pg = len(page_specs)
    return pl.pallas_call(
        _moba_sample_attn_kernel, grid_spec=gs,
        out_shape=jax.ShapeDtypeStruct((nbat, hh, 1, d), F32),
        compiler_params=_params("parallel", "arbitrary"),
        name="moba_sample_attn",
    )(page_table, idx, q, k_new, v_new, *([pool_k] * n_pg), *([pool_v] * n_pg))


def _cross_sample_kernel(q_ref, mk_ref, mv_ref, o_ref):
    q = q_ref[0]
    outs = []
    for hd in range(CROSS_HEADS):
        sl = slice(hd * HEAD_DIM, (hd + 1) * HEAD_DIM)
        q16 = jnp.broadcast_to(q[:, sl], (8, HEAD_DIM)).astype(BF16)
        s = _dot_nt(q16, mk_ref[0, :, sl].astype(BF16)) * HEAD_DIM ** -0.5
        e = jnp.exp(s - jnp.max(s, axis=-1, keepdims=True))
        p = e / jnp.sum(e, axis=-1, keepdims=True)
        outs.append(_dot(p.astype(BF16), mv_ref[0, :, sl].astype(BF16))[0:1, :])
    o_ref[0] = jnp.concatenate(outs, axis=-1)


def _cross_sample(q, mem_k, mem_v):
    nbat = q.shape[0]
    row = lambda a: pl.BlockSpec((1,) + a.shape[1:], lambda b: (b,) + (0,) * (a.ndim - 1))
    return pl.pallas_call(
        _cross_sample_kernel,
        grid=(nbat,),
        in_specs=[row(q), row(mem_k), row(mem_v)],
        out_specs=row(q),
        out_shape=jax.ShapeDtypeStruct(q.shape, F32),
        compiler_params=_params("parallel"),
        name="cross_sample",
    )(q, mem_k, mem_v)


def _rope_tables(pos):
    half = ROPE_DIM // 2
    inv_freq = ROPE_THETA ** (-jnp.arange(0, ROPE_DIM, 2, dtype=F32) / ROPE_DIM)
    ang = pos.astype(F32)[:, None] * inv_freq[None, :]
    cos, sin = jnp.cos(ang), jnp.sin(ang)
    n = pos.shape[0]
    cos_t = jnp.concatenate([cos, cos, jnp.ones((n, HEAD_DIM - ROPE_DIM), F32)], axis=-1)
    sin_t = jnp.concatenate([-sin, sin, jnp.zeros((n, HEAD_DIM - ROPE_DIM), F32)], axis=-1)
    return cos_t, sin_t


def kernel(x_prompt, x_sample, cache_moba_k, cache_moba_v, page_table, state_gdn, state_conv, cache_mem_k, cache_mem_v, mem_prompt, w_ffn1_norm, w_ffn1_gu, w_ffn1_down, w_mix_norm, w_in, gdn_conv_w, gdn_a_log, gdn_dt_bias, gdn_out_norm, w_out, w_cross_norm, w_cross_q, w_cross_out, w_mem_norm, w_mem_kv, w_ffn2_norm, w_ffn2_gu, w_ffn2_down, w_final_norm):
    assert x_prompt.shape[0] == 1 and x_sample.shape[1] == 1 and w_in.shape[0] == 1
    s_len, d_model = x_prompt.shape[1], x_prompt.shape[2]
    nbat = x_sample.shape[0]
    past_len = page_table.shape[1] * PAGE_SIZE
    assert past_len % MOBA_BLOCK == 0

    row = lambda v: v.reshape(1, -1).astype(F32)
    g_ffn1, g_mix, g_cross, g_mem, g_ffn2 = (row(w[0]) for w in (w_ffn1_norm, w_mix_norm, w_cross_norm, w_mem_norm, w_ffn2_norm))
    g_final = row(w_final_norm)
    gu1, dn1 = w_ffn1_gu[0].astype(BF16), w_ffn1_down[0].astype(BF16)
    gu2, dn2 = w_ffn2_gu[0].astype(BF16), w_ffn2_down[0].astype(BF16)
    gdn_end = 4 * GDN_WIDTH
    ba_end = gdn_end + 2 * GDN_HEADS
    w_main = jnp.concatenate([w_in[0][:, :gdn_end], w_in[0][:, ba_end:]], axis=1).astype(BF16)
    w_ba = jnp.pad(w_in[0][:, gdn_end:ba_end], ((0, 0), (0, LANES - 2 * GDN_HEADS))).astype(BF16)
    wo_gdn, wo_moba = w_out[0][:GDN_WIDTH].astype(BF16), w_out[0][GDN_WIDTH:].astype(BF16)
    wq_c, wo_c = w_cross_q[0].astype(BF16), w_cross_out[0].astype(BF16)
    w_kv = w_mem_kv[0].astype(BF16)
    conv_w = gdn_conv_w[0]
    lane_pad = lambda v: jnp.pad(v.reshape(1, -1).astype(F32), ((0, 0), (GDN_HEADS, LANES - 2 * GDN_HEADS)))
    alog_row, dtb_row = lane_pad(gdn_a_log[0]), lane_pad(gdn_dt_bias[0])
    onorm = row(gdn_out_norm[0])
    moba_col0 = gdn_end

    xp = x_prompt[0]
    tm = _tile(s_len, 512)
    tm_ffn = _tile(s_len, 512)
    h = _ffn(xp, g_ffn1, gu1, dn1, g_final, final_norm=False, tm=tm_ffn, tf=512)
    cols = _rms_matmul(h, g_mix, w_main, tm=_tile(s_len, 1024), tn=1024)
    ba = _rms_matmul(h, g_mix, w_ba, tm=_tile(s_len, 1024), tn=LANES)
    o_gdn, p_gdn_state, conv_tail = _gdn_prompt(cols, ba, conv_w, alog_row, dtb_row, onorm, chunks_per_step=2)
    cos_p, sin_p = _rope_tables(jnp.arange(s_len))
    q16, p_k, k16, p_v, vt16, bias = _moba_prep(cols, cos_p, sin_p, moba_col0)
    o_moba = _moba_attn(q16, k16, vt16, bias, hg=4)
    h = _proj_res(h, [(o_gdn, wo_gdn), (o_moba, wo_moba)], tm=tm)
    mem_kv = _rms_matmul(mem_prompt[0], g_mem, w_kv, tm=mem_prompt.shape[1], tn=2 * CROSS_WIDTH)
    mem_k, mem_v = mem_kv[:, :CROSS_WIDTH], mem_kv[:, CROSS_WIDTH:]
    h = _cross_prompt(h, g_cross, wq_c, mem_k.astype(BF16), mem_v.astype(BF16), wo_c, tm=tm)
    y_prompt = _ffn(h, g_ffn2, gu2, dn2, g_final, final_norm=True, tm=tm_ffn, tf=512)

    xs = x_sample[:, 0]
    hs = _ffn(xs, g_ffn1, gu1, dn1, g_final, final_norm=False, tm=nbat, tf=512)
    cols_s = _rms_matmul(hs, g_mix, w_main, tm=nbat, tn=1024)
    ba_s = _rms_matmul(hs, g_mix, w_ba, tm=nbat, tn=LANES)
    o_gdn_s, s_gdn_state, s_conv = _gdn_sample(
        cols_s[:, None, :GDN_CONV_CH], state_conv[0], cols_s[:, None, GDN_CONV_CH:gdn_end], ba_s[:, None, :],
        conv_w, alog_row, dtb_row, onorm, state_gdn[0])
    mq, mk, mv = (cols_s[:, moba_col0 + j * MOBA_WIDTH:moba_col0 + (j + 1) * MOBA_WIDTH].reshape(nbat, MOBA_HEADS, HEAD_DIM) for j in range(3))
    kmean = _moba_kmean(page_table, cache_moba_k)
    cos_s, sin_s = _rope_tables(jnp.full((1,), past_len))
    n_sel = min(MOBA_TOPK, past_len // MOBA_BLOCK)
    q_r, k_r, idx = _moba_sample_gate(mq, mk, cos_s, sin_s, kmean, n_sel)
    o_moba_s = _moba_sample_attn(page_table, idx, q_r[:, :, None, :], k_r[:, :, None, :], mv[:, :, None, :],
                                 cache_moba_k, cache_moba_v)
    hs = _proj_res(hs, [(o_gdn_s[:, 0], wo_gdn), (o_moba_s.reshape(nbat, MOBA_WIDTH), wo_moba)], tm=nbat)
    q_c = _rms_matmul(hs, g_cross, wq_c, tm=nbat, tn=CROSS_WIDTH)
    mem_tokens = cache_mem_k.shape[2]
    o_c = _cross_sample(q_c[:, None, :], cache_mem_k[0].reshape(nbat, mem_tokens, CROSS_WIDTH),
                        cache_mem_v[0].reshape(nbat, mem_tokens, CROSS_WIDTH))
    hs = _proj_res(hs, [(o_c[:, 0], wo_c)], tm=nbat)
    y_sample = _ffn(hs, g_ffn2, gu2, dn2, g_final, final_norm=True, tm=nbat, tf=512)

    return (y_prompt[None], y_sample[:, None, :],
            p_k[None, None], p_v[None, None],
            p_gdn_state[None, None], conv_tail[8 - (CONV_WIDTH - 1):][None, None],
            mem_k.reshape(1, 1, -1, CROSS_HEADS, HEAD_DIM), mem_v.reshape(1, 1, -1, CROSS_HEADS, HEAD_DIM),
            k_r[None, :, :, None, :], mv[None, :, :, None, :],
            s_gdn_state[None], s_conv[None])
```

```python
import functools

import jax
import jax.numpy as jnp
from jax import lax
from jax.experimental import pallas as pl
from jax.experimental.pallas import tpu as pltpu

F32 = jnp.float32
BF16 = jnp.bfloat16
HIGHEST = lax.Precision.HIGHEST

HEAD_DIM = 128
GDN_HEADS = 8
MOBA_HEADS = 8
GDN_WIDTH = GDN_HEADS * HEAD_DIM
MOBA_WIDTH = MOBA_HEADS * HEAD_DIM
CONV_WIDTH = 4
GDN_CONV_CH = 3 * GDN_WIDTH
GDN_CHUNK = 64
MOBA_BLOCK = 256
MOBA_TOPK = 3
PAGE_SIZE = 128
ROPE_DIM = HEAD_DIM // 4
ROPE_THETA = 500000.0
CROSS_HEADS = 4
CROSS_WIDTH = CROSS_HEADS * HEAD_DIM
NORM_EPS = 1e-6

LANES = 128
MASK_VALUE = -1e30
LOG2_E = 1.4426950408889634
VT_ROWS = HEAD_DIM + 16
VMEM_LIMIT_BYTES = 56 * 1024 * 1024

_NT = (((1,), (1,)), ((), ()))
_TN = (((0,), (0,)), ((), ()))


def _params(*sem):
    return pltpu.CompilerParams(dimension_semantics=sem, vmem_limit_bytes=VMEM_LIMIT_BYTES)


def _rms(x, g):
    return x * lax.rsqrt(jnp.mean(x * x, axis=-1, keepdims=True) + NORM_EPS) * g


def _silu(x):
    return x * jax.nn.sigmoid(x)


def _dot(a, b):
    return jnp.dot(a, b, preferred_element_type=F32)


def _dot_nt(a, b):
    return lax.dot_general(a, b, _NT, preferred_element_type=F32)


def _dot_f32(a, b):
    return jnp.dot(a, b, precision=HIGHEST, preferred_element_type=F32)


def _col(x, c, width):
    return jnp.broadcast_to(x[:, c:c + 1], (x.shape[0], width))


def _tile(m, pref):
    return pref if m % pref == 0 else m


def _rms_matmul_kernel(x_ref, g_ref, w_ref, o_ref, xn_ref):
    @pl.when(pl.program_id(1) == 0)
    def _():
        xn_ref[...] = _rms(x_ref[...], g_ref[...]).astype(BF16)

    o_ref[...] = _dot(xn_ref[...], w_ref[...])


def _rms_matmul(x, g, w, *, tm, tn):
    m, d = x.shape
    n = w.shape[1]
    return pl.pallas_call(
        _rms_matmul_kernel,
        grid=(m // tm, n // tn),
        in_specs=[pl.BlockSpec((tm, d), lambda i, j: (i, 0)),
                  pl.BlockSpec((1, d), lambda i, j: (0, 0)),
                  pl.BlockSpec((d, tn), lambda i, j: (0, j))],
        out_specs=pl.BlockSpec((tm, tn), lambda i, j: (i, j)),
        out_shape=jax.ShapeDtypeStruct((m, n), F32),
        scratch_shapes=[pltpu.VMEM((tm, d), BF16)],
        compiler_params=_params("parallel", "arbitrary"),
        name="rms_matmul",
    )(x, g, w)


_PAGES_PER_STEP = 8


def _block_key_means(page_refs, o_ref):
    ppb = MOBA_BLOCK // PAGE_SIZE
    for blk in range(len(page_refs) // ppb):
        tot = jnp.sum(page_refs[blk * ppb][0, 0], axis=1)
        for r in range(1, ppb):
            tot = tot + jnp.sum(page_refs[blk * ppb + r][0, 0], axis=1)
        o_ref[0, blk] = tot * (1.0 / MOBA_BLOCK)


def _ffn_kernel(*refs, final_norm, n_pages):
    if n_pages:
        refs = refs[1:]
    x_ref, g_ref, wg_ref, wu_ref, wd_ref, fg_ref = refs[:6]
    page_refs = refs[6:6 + n_pages]
    o_ref, xn_ref = refs[6 + n_pages], refs[-1]
    j = pl.program_id(1)

    @pl.when(j == 0)
    def _():
        xn_ref[...] = _rms(x_ref[...], g_ref[...]).astype(BF16)
        o_ref[...] = jnp.zeros_like(o_ref)

    xn = xn_ref[...]
    act = _silu(_dot(xn, wg_ref[...])) * _dot(xn, wu_ref[...])
    o_ref[...] += _dot(act.astype(BF16), wd_ref[...])
    if n_pages:
        _block_key_means(page_refs, refs[7 + n_pages])

    @pl.when(j == pl.num_programs(1) - 1)
    def _():
        h = x_ref[...] + 0.5 * o_ref[...]
        if final_norm:
            h = _rms(h, fg_ref[...])
        o_ref[...] = h


def _kmean_slots(page_table_rows, n_pages):
    return page_table_rows * (n_pages // _PAGES_PER_STEP)


def _ffn(x, g, w_gu, w_down, fg, *, final_norm, tm, tf, paged=None):
    m, d = x.shape
    f = w_down.shape[0]
    nf = f // tf
    grid = (m // tm, nf)
    in_specs = [pl.BlockSpec((tm, d), lambda i, j, *_: (i, 0)),
                pl.BlockSpec((1, d), lambda i, j, *_: (0, 0)),
                pl.BlockSpec((d, tf), lambda i, j, *_: (0, j)),
                pl.BlockSpec((d, tf), lambda i, j, *_: (0, nf + j)),
                pl.BlockSpec((tf, d), lambda i, j, *_: (j, 0)),
                pl.BlockSpec((1, d), lambda i, j, *_: (0, 0))]
    out_spec = pl.BlockSpec((tm, d), lambda i, j, *_: (i, 0))
    out_shape = jax.ShapeDtypeStruct((m, d), F32)
    scratch = [pltpu.VMEM((tm, d), BF16)]
    if paged is None:
        return pl.pallas_call(
            functools.partial(_ffn_kernel, final_norm=final_norm, n_pages=0),
            grid=grid, in_specs=in_specs, out_specs=out_spec, out_shape=out_shape, scratch_shapes=scratch,
            compiler_params=_params("parallel", "arbitrary"), name="ffn",
        )(x, g, w_gu, w_gu, w_down, fg)

    page_table, pool_k, row0, n_rows = paged
    n_pages = page_table.shape[1]
    _, _, hh, ps, hd = pool_k.shape
    groups = n_pages // _PAGES_PER_STEP
    n_slots = _kmean_slots(n_rows, n_pages)
    assert n_pages % _PAGES_PER_STEP == 0 and n_slots <= grid[0] * grid[1]
    ppb = MOBA_BLOCK // PAGE_SIZE

    def slot(i, j):
        t = jnp.minimum(i * nf + j, n_slots - 1)
        return lax.div(t, groups), lax.rem(t, groups)

    def page_map(i, j, pt, *, r):
        b, c = slot(i, j)
        return (0, pt[row0 + b, c * _PAGES_PER_STEP + r], 0, 0, 0)

    def kmean_map(i, j, pt):
        b, c = slot(i, j)
        return (b, c, 0, 0)

    page_specs = [pl.BlockSpec((1, 1, hh, ps, hd), functools.partial(page_map, r=r)) for r in range(_PAGES_PER_STEP)]
    gs = pltpu.PrefetchScalarGridSpec(
        num_scalar_prefetch=1, grid=grid, in_specs=in_specs + page_specs,
        out_specs=[out_spec, pl.BlockSpec((1, _PAGES_PER_STEP // ppb, hh, hd), kmean_map)],
        scratch_shapes=scratch)
    return pl.pallas_call(
        functools.partial(_ffn_kernel, final_norm=final_norm, n_pages=_PAGES_PER_STEP),
        grid_spec=gs,
        out_shape=[out_shape, jax.ShapeDtypeStruct((n_rows, n_pages // ppb, hh, hd), F32)],
        compiler_params=_params("arbitrary", "arbitrary"), name="ffn_kmean",
    )(page_table, x, g, w_gu, w_gu, w_down, fg, *([pool_k] * _PAGES_PER_STEP))


def _proj_res_kernel(*refs):
    res_ref, o_ref = refs[0], refs[-1]
    pairs = refs[1:-1]
    acc = res_ref[...]
    for a_ref, w_ref in zip(pairs[0::2], pairs[1::2]):
        acc = acc + _dot(a_ref[...].astype(BF16), w_ref[...])
    o_ref[...] = acc


def _proj_res(res, pairs, *, tm):
    m, d = res.shape
    in_specs = [pl.BlockSpec((tm, d), lambda i: (i, 0))]
    args = [res]
    for a, w in pairs:
        in_specs.append(pl.BlockSpec((tm, a.shape[1]), lambda i: (i, 0)))
        in_specs.append(pl.BlockSpec(w.shape, lambda i: (0, 0)))
        args += [a, w]
    return pl.pallas_call(
        _proj_res_kernel,
        grid=(m // tm,),
        in_specs=in_specs,
        out_specs=pl.BlockSpec((tm, d), lambda i: (i, 0)),
        out_shape=jax.ShapeDtypeStruct((m, d), F32),
        compiler_params=_params("parallel"),
        name="proj_res",
    )(*args)


def _cross_kernel(h_ref, g_ref, wq_ref, mk_ref, mv_ref, wo_ref, o_ref):
    h = h_ref[...]
    q = _dot(_rms(h, g_ref[...]).astype(BF16), wq_ref[...])
    outs = []
    for hd in range(CROSS_HEADS):
        sl = slice(hd * HEAD_DIM, (hd + 1) * HEAD_DIM)
        s = _dot_nt(q[:, sl].astype(BF16), mk_ref[:, sl]) * HEAD_DIM ** -0.5
        e = jnp.exp(s - jnp.max(s, axis=-1, keepdims=True))
        p = e / jnp.sum(e, axis=-1, keepdims=True)
        outs.append(_dot(p.astype(BF16), mv_ref[:, sl]).astype(BF16))
    o_ref[...] = h + _dot(jnp.concatenate(outs, axis=-1), wo_ref[...])


def _cross_prompt(h, g, wq, mem_k, mem_v, wo, *, tm):
    m, d = h.shape
    full = lambda a: pl.BlockSpec(a.shape, lambda i: (0, 0))
    return pl.pallas_call(
        _cross_kernel,
        grid=(m // tm,),
        in_specs=[pl.BlockSpec((tm, d), lambda i: (i, 0)), full(g), full(wq), full(mem_k), full(mem_v), full(wo)],
        out_specs=pl.BlockSpec((tm, d), lambda i: (i, 0)),
        out_shape=jax.ShapeDtypeStruct((m, d), F32),
        compiler_params=_params("parallel"),
        name="cross_prompt",
    )(h, g, wq, mem_k, mem_v, wo)


def _split(x):
    hi = x.astype(BF16)
    return hi, (x - hi.astype(F32)).astype(BF16)


def _dot_split(a, b):
    return _dot(a[0], b[0]) + _dot(a[0], b[1]) + _dot(a[1], b[0])


def _gdn_prompt_kernel(qkv_ref, z_ref, ba_ref, cw_ref, alog_ref, dtb_ref, onorm_ref,
                       o_ref, sfin_ref, convn_ref, xp_ref, y_ref, s_ref):
    c = GDN_CHUNK
    rows = qkv_ref.shape[0]
    nck = rows // c
    i = pl.program_id(0)

    @pl.when(i == 0)
    def _():
        xp_ref[0:8, :] = jnp.zeros((8, GDN_CONV_CH), F32)
        s_ref[...] = jnp.zeros_like(s_ref)

    xp_ref[8:8 + rows, :] = qkv_ref[...]
    y = cw_ref[0:1, :] * xp_ref[5:5 + rows, :]
    for j in range(1, CONV_WIDTH):
        y = y + cw_ref[j:j + 1, :] * xp_ref[5 + j:5 + j + rows, :]
    y_ref[...] = _silu(y)
    tail = xp_ref[rows:rows + 8, :]
    xp_ref[0:8, :] = tail
    convn_ref[...] = tail

    ba = ba_ref[...]
    beta = jax.nn.sigmoid(ba)
    g = -jnp.exp(alog_ref[...]) * jax.nn.softplus(ba + dtb_ref[...])
    row = lax.broadcasted_iota(jnp.int32, (c, c), 0)
    colv = lax.broadcasted_iota(jnp.int32, (c, c), 1)
    causal = row >= colv
    strict = row > colv
    eye = (row == colv).astype(F32)
    tri = causal.astype(F32)
    heads = range(GDN_HEADS)

    work = []
    for ck in range(nck):
        r0 = ck * c
        gc = _dot_f32(tri, g[r0:r0 + c])
        gc_t = gc.T
        eg = jnp.exp(gc)
        g_last = gc[c - 1:c, :]
        eg_last = jnp.exp(g_last)
        k_decay = jnp.exp(g_last - gc)
        for h in heads:
            gh = GDN_HEADS + h
            qh = y_ref[r0:r0 + c, h * HEAD_DIM:(h + 1) * HEAD_DIM]
            kh = y_ref[r0:r0 + c, GDN_WIDTH + h * HEAD_DIM:GDN_WIDTH + (h + 1) * HEAD_DIM]
            vh = y_ref[r0:r0 + c, 2 * GDN_WIDTH + h * HEAD_DIM:2 * GDN_WIDTH + (h + 1) * HEAD_DIM]
            qn = qh * lax.rsqrt(jnp.sum(qh * qh, axis=-1, keepdims=True) + NORM_EPS) * HEAD_DIM ** -0.5
            kn = kh * lax.rsqrt(jnp.sum(kh * kh, axis=-1, keepdims=True) + NORM_EPS)
            b_col = _col(beta[r0:r0 + c], h, HEAD_DIM)
            eg_col = _col(eg, gh, HEAD_DIM)
            diff = _col(gc, gh, c) - jnp.broadcast_to(gc_t[gh:gh + 1, :], (c, c))
            decay = jnp.exp(jnp.where(causal, diff, -jnp.inf))
            kb = kn * b_col
            kn16 = kn.astype(BF16)
            qn16 = qn.astype(BF16)
            work.append(dict(
                a=_dot_nt(kb.astype(BF16), kn16) * jnp.where(strict, decay, 0.0),
                attn=(_dot_nt(qn16, kn16) * decay).astype(BF16),
                vb=(vh * b_col).astype(BF16),
                kbg=(kb * eg_col).astype(BF16),
                qg=(qn * eg_col).astype(BF16),
                kd=(kn * _col(k_decay, gh, HEAD_DIM)).astype(BF16),
                eg_last=eg_last[:, gh:gh + 1]))

    ps = [_split(-wk["a"]) for wk in work]
    ts = [eye - wk["a"] for wk in work]
    for _ in range(5):
        ps = [_split(_dot_split(p, p)) for p in ps]
        ts = [t + _dot_split(_split(t), p) for t, p in zip(ts, ps)]
    for wk, t in zip(work, ts):
        t16 = t.astype(BF16)
        wk["u"] = _dot(t16, wk["vb"])
        wk["w"] = _dot(t16, wk["kbg"]).astype(BF16)

    for ck in range(nck):
        r0 = ck * c
        wks = work[ck * GDN_HEADS:(ck + 1) * GDN_HEADS]
        states = [s_ref[h] for h in heads]
        s16 = [s.astype(BF16) for s in states]
        v_new = [(wk["u"] - _dot(wk["w"], s16[h])).astype(BF16) for h, wk in zip(heads, wks)]
        outs = [_dot(wk["qg"], s16[h]) + _dot(wk["attn"], v_new[h]) for h, wk in zip(heads, wks)]
        for h, wk in zip(heads, wks):
            s_ref[h] = states[h] * wk["eg_last"] + lax.dot_general(wk["kd"], v_new[h], _TN, preferred_element_type=F32)
        for h, o in zip(heads, outs):
            sl = slice(h * HEAD_DIM, (h + 1) * HEAD_DIM)
            o = o * lax.rsqrt(jnp.mean(o * o, axis=-1, keepdims=True) + NORM_EPS) * onorm_ref[...]
            o_ref[r0:r0 + c, sl] = o * _silu(z_ref[r0:r0 + c, sl])

    @pl.when(i == pl.num_programs(0) - 1)
    def _():
        sfin_ref[...] = s_ref[...]


def _gdn_prompt(cols, ba, conv_w, alog_row, dtb_row, onorm, *, chunks_per_step):
    s_len = cols.shape[0]
    c = GDN_CHUNK * chunks_per_step
    assert s_len % c == 0
    full = lambda a: pl.BlockSpec(a.shape, lambda i: (0,) * a.ndim)
    return pl.pallas_call(
        _gdn_prompt_kernel,
        grid=(s_len // c,),
        in_specs=[pl.BlockSpec((c, GDN_CONV_CH), lambda i: (i, 0)),
                  pl.BlockSpec((c, GDN_WIDTH), lambda i: (i, GDN_CONV_CH // GDN_WIDTH)),
                  pl.BlockSpec((c, LANES), lambda i: (i, 0)),
                  full(conv_w), full(alog_row), full(dtb_row), full(onorm)],
        out_specs=[pl.BlockSpec((c, GDN_WIDTH), lambda i: (i, 0)),
                   pl.BlockSpec((GDN_HEADS, HEAD_DIM, HEAD_DIM), lambda i: (0, 0, 0)),
                   pl.BlockSpec((8, GDN_CONV_CH), lambda i: (0, 0))],
        out_shape=[jax.ShapeDtypeStruct((s_len, GDN_WIDTH), F32),
                   jax.ShapeDtypeStruct((GDN_HEADS, HEAD_DIM, HEAD_DIM), F32),
                   jax.ShapeDtypeStruct((8, GDN_CONV_CH), F32)],
        scratch_shapes=[pltpu.VMEM((c + 8, GDN_CONV_CH), F32),
                        pltpu.VMEM((c, GDN_CONV_CH), F32),
                        pltpu.VMEM((GDN_HEADS, HEAD_DIM, HEAD_DIM), F32)],
        compiler_params=_params("arbitrary"),
        name="gdn_prompt",
    )(cols, cols, ba, conv_w, alog_row, dtb_row, onorm)


def _gdn_sample_kernel(x_ref, cs_ref, z_ref, ba_ref, cw_ref, alog_ref, dtb_ref, onorm_ref, s0_ref,
                       o_ref, s_ref, convn_ref):
    x = x_ref[0]
    cs = cs_ref[0]
    y = cw_ref[CONV_WIDTH - 1:CONV_WIDTH, :] * x
    for j in range(CONV_WIDTH - 1):
        y = y + cw_ref[j:j + 1, :] * cs[j:j + 1, :]
    y = _silu(y)
    convn_ref[0] = jnp.concatenate([cs[1:CONV_WIDTH - 1, :], x], axis=0)

    ba = ba_ref[0]
    beta = jax.nn.sigmoid(ba)
    eg = jnp.exp(-jnp.exp(alog_ref[...]) * jax.nn.softplus(ba + dtb_ref[...]))
    z = z_ref[0]

    def head_rows(base):
        return jnp.concatenate([y[:, base + h * HEAD_DIM:base + (h + 1) * HEAD_DIM] for h in range(GDN_HEADS)], axis=0)

    q8 = head_rows(0)
    k8 = head_rows(GDN_WIDTH)
    v8 = head_rows(2 * GDN_WIDTH)
    q8 = q8 * lax.rsqrt(jnp.sum(q8 * q8, axis=-1, keepdims=True) + NORM_EPS) * HEAD_DIM ** -0.5
    k8 = k8 * lax.rsqrt(jnp.sum(k8 * k8, axis=-1, keepdims=True) + NORM_EPS)
    qk8 = jnp.sum(q8 * k8, axis=-1, keepdims=True)
    q_t = q8.T
    k_t = k8.T
    outs = []
    for h in range(GDN_HEADS):
        s = s0_ref[0, h]
        k_col = _col(k_t, h, HEAD_DIM)
        q_col = _col(q_t, h, HEAD_DIM)
        e = eg[:, GDN_HEADS + h:GDN_HEADS + h + 1]
        k_s = jnp.sum(k_col * s, axis=0, keepdims=True)
        q_s = jnp.sum(q_col * s, axis=0, keepdims=True)
        v_new = beta[:, h:h + 1] * (v8[h:h + 1, :] - e * k_s)
        o = e * q_s + qk8[h:h + 1, :] * v_new
        s_ref[0, h] = s * e + k_col * v_new
        o = o * lax.rsqrt(jnp.mean(o * o, axis=-1, keepdims=True) + NORM_EPS) * onorm_ref[...]
        outs.append(o * _silu(z[:, h * HEAD_DIM:(h + 1) * HEAD_DIM]))
    o_ref[0] = jnp.concatenate(outs, axis=-1)


def _gdn_sample(x, conv_state, z, ba, conv_w, alog_row, dtb_row, onorm, s0):
    nb = x.shape[0]
    full = lambda a: pl.BlockSpec(a.shape, lambda b: (0,) * a.ndim)
    row = lambda a: pl.BlockSpec((1,) + a.shape[1:], lambda b: (b,) + (0,) * (a.ndim - 1))
    return pl.pallas_call(
        _gdn_sample_kernel,
        grid=(nb,),
        in_specs=[row(x), row(conv_state), row(z), row(ba), full(conv_w), full(alog_row), full(dtb_row),
                  full(onorm), row(s0)],
        out_specs=[pl.BlockSpec((1, 1, GDN_WIDTH), lambda b: (b, 0, 0)), row(s0), row(conv_state)],
        out_shape=[jax.ShapeDtypeStruct((nb, 1, GDN_WIDTH), F32),
                   jax.ShapeDtypeStruct(s0.shape, F32),
                   jax.ShapeDtypeStruct(conv_state.shape, F32)],
        compiler_params=_params("parallel"),
        name="gdn_sample",
    )(x, conv_state, z, ba, conv_w, alog_row, dtb_row, onorm, s0)


def _rope(x, cos, sin_signed):
    lane = lax.broadcasted_iota(jnp.int32, x.shape, x.ndim - 1)
    half = ROPE_DIM // 2
    partner = jnp.where(lane < half, pltpu.roll(x, LANES - half, x.ndim - 1), pltpu.roll(x, half, x.ndim - 1))
    return x * cos + partner * sin_signed


def _moba_prep_kernel(q_ref, k_ref, v_ref, cos_ref, sin_ref,
                      q16_ref, kf_ref, k16_ref, vf_ref, vt16_ref, bias_ref, kmean_ref, *, n_sel):
    i = pl.program_id(0)
    nb = kmean_ref.shape[1]
    t = q_ref.shape[0]

    @pl.when(i == 0)
    def _():
        kmean_ref[...] = jnp.zeros_like(kmean_ref)

    cos = cos_ref[...]
    sin = sin_ref[...]
    blk = lax.broadcasted_iota(jnp.int32, (nb, t), 0)
    past = blk < i
    for h in range(MOBA_HEADS):
        sl = slice(h * HEAD_DIM, (h + 1) * HEAD_DIM)
        q = _rope(q_ref[:, sl], cos, sin)
        k = _rope(k_ref[:, sl], cos, sin)
        v = v_ref[:, sl]
        q16_ref[h] = (q * (HEAD_DIM ** -0.5 * LOG2_E)).astype(BF16)
        kf_ref[h] = k
        k16_ref[h] = k.astype(BF16)
        vf_ref[h] = v
        vt16_ref[h, 0, 0:HEAD_DIM, :] = v.T.astype(BF16)
        vt16_ref[h, 0, HEAD_DIM:VT_ROWS, :] = (lax.broadcasted_iota(jnp.int32, (VT_ROWS - HEAD_DIM, t), 0) == 0).astype(BF16)
        gate = lax.dot_general(kmean_ref[h], q, _NT, precision=HIGHEST, preferred_element_type=F32)
        gate = jnp.where(past, gate, -jnp.inf)
        chosen = jnp.zeros((nb, t), F32)
        for _ in range(n_sel):
            top = jnp.max(gate, axis=0, keepdims=True)
            first = jnp.min(jnp.where(gate == top, blk, nb), axis=0, keepdims=True)
            pick = (blk == first) & (top > -jnp.inf)
            chosen = jnp.where(pick, 1.0, chosen)
            gate = jnp.where(pick, -jnp.inf, gate)
        bias_ref[h] = jnp.where(chosen > 0.0, 0.0, MASK_VALUE)
        kmean_ref[h, pl.ds(i, 1), :] = jnp.mean(k, axis=0, keepdims=True)


def _moba_prep(cols, cos, sin, col0):
    s_len = cols.shape[0]
    t = MOBA_BLOCK
    assert s_len % t == 0
    nb = s_len // t
    hh = MOBA_HEADS
    cb = col0 // MOBA_WIDTH
    n_sel = min(MOBA_TOPK, nb - 1)
    return pl.pallas_call(
        functools.partial(_moba_prep_kernel, n_sel=n_sel),
        grid=(nb,),
        in_specs=[pl.BlockSpec((t, MOBA_WIDTH), lambda i: (i, cb)),
                  pl.BlockSpec((t, MOBA_WIDTH), lambda i: (i, cb + 1)),
                  pl.BlockSpec((t, MOBA_WIDTH), lambda i: (i, cb + 2)),
                  pl.BlockSpec((t, LANES), lambda i: (i, 0)),
                  pl.BlockSpec((t, LANES), lambda i: (i, 0))],
        out_specs=[pl.BlockSpec((hh, t, HEAD_DIM), lambda i: (0, i, 0)),
                   pl.BlockSpec((hh, t, HEAD_DIM), lambda i: (0, i, 0)),
                   pl.BlockSpec((hh, t, HEAD_DIM), lambda i: (0, i, 0)),
                   pl.BlockSpec((hh, t, HEAD_DIM), lambda i: (0, i, 0)),
                   pl.BlockSpec((hh, 1, VT_ROWS, t), lambda i: (0, i, 0, 0)),
                   pl.BlockSpec((hh, nb, t), lambda i: (0, 0, i))],
        out_shape=[jax.ShapeDtypeStruct((hh, s_len, HEAD_DIM), BF16),
                   jax.ShapeDtypeStruct((hh, s_len, HEAD_DIM), F32),
                   jax.ShapeDtypeStruct((hh, s_len, HEAD_DIM), BF16),
                   jax.ShapeDtypeStruct((hh, s_len, HEAD_DIM), F32),
                   jax.ShapeDtypeStruct((hh, nb, VT_ROWS, t), BF16),
                   jax.ShapeDtypeStruct((hh, nb, s_len), F32)],
        scratch_shapes=[pltpu.VMEM((hh, nb, HEAD_DIM), F32)],
        compiler_params=_params("arbitrary"),
        name="moba_prep",
    )(cols, cols, cols, cos, sin)


_KV_PAIR = 2


def _moba_attn_kernel(q_ref, k_ref, vt_ref, bias_ref, o_ref, acc_ref, s_ref, p_ref):
    hg, tq, _ = q_ref.shape
    tk = vt_ref.shape[3]
    own = pl.program_id(1)
    diag = own // _KV_PAIR
    kpos = lax.broadcasted_iota(jnp.int32, (tk, tq), 0)
    qpos = lax.broadcasted_iota(jnp.int32, (tk, tq), 1)
    causal_bias = jnp.where(kpos <= qpos, 0.0, MASK_VALUE)

    def scores(j, slot, diagonal=False):
        g = diag if diagonal else jnp.minimum(j - 1, diag)
        real = j <= diag
        off = pl.multiple_of(g * (_KV_PAIR * tk), _KV_PAIR * tk)
        for h in range(hg):
            s = _dot_nt(k_ref[h, pl.ds(off, _KV_PAIR * tk), :], q_ref[h])
            for r in range(_KV_PAIR):
                kb = g * _KV_PAIR + r
                if diagonal:
                    row = jnp.broadcast_to(bias_ref[h, pl.ds(kb, 1), :], (tk, tq))
                    bias = jnp.where(kb == own, causal_bias, row)
                else:
                    bias = jnp.where(real, bias_ref[h, pl.ds(kb, 1), :], MASK_VALUE)
                s_ref[slot, h, r * tk:(r + 1) * tk, :] = s[r * tk:(r + 1) * tk] + bias

    def softmax(slot, ms):
        new, alphas = [], []
        for h in range(hg):
            s = s_ref[slot, h]
            m_new = jnp.maximum(ms[h], jnp.max(s, axis=0, keepdims=True))
            p_ref[slot, h] = jnp.exp2(s - m_new).astype(BF16)
            new.append(m_new)
            alphas.append(jnp.exp2(ms[h] - m_new))
        return tuple(new), tuple(alphas)

    def accumulate(j, slot, alphas):
        g = jnp.where(j == 0, diag, jnp.minimum(j - 1, diag))
        for h in range(hg):
            pv = _dot(vt_ref[h, g * _KV_PAIR], p_ref[slot, h, 0:tk, :])
            for r in range(1, _KV_PAIR):
                pv = pv + _dot(vt_ref[h, g * _KV_PAIR + r], p_ref[slot, h, r * tk:(r + 1) * tk, :])
            acc_ref[h] = acc_ref[h] * alphas[h] + pv

    acc_ref[...] = jnp.zeros_like(acc_ref)
    m0 = jnp.full((1, tq), MASK_VALUE, F32)
    scores(0, 0, diagonal=True)
    scores(1, 1)
    carry = softmax(0, (m0,) * hg)

    def step(j, cur, carry):
        ms, alphas = carry
        scores(j + 1, 1 - cur)
        new = softmax(cur, ms)
        accumulate(j - 1, 1 - cur, alphas)
        return new

    def two_steps(t, carry):
        return step(2 * t + 2, 0, step(2 * t + 1, 1, carry))

    half = (diag + 1) // 2
    _, alphas = lax.fori_loop(0, half, two_steps, carry)
    accumulate(2 * half, 0, alphas)
    for h in range(hg):
        acc = acc_ref[h]
        o_ref[:, h * HEAD_DIM:(h + 1) * HEAD_DIM] = (acc[0:HEAD_DIM] / acc[HEAD_DIM:HEAD_DIM + 1]).T


def _moba_attn(q16, k16, vt16, bias, *, hg):
    hh, s_len, d = q16.shape
    nb, rows, tk = vt16.shape[1:]
    assert nb % _KV_PAIR == 0 and hh % hg == 0
    once = pl.Buffered(1)
    return pl.pallas_call(
        _moba_attn_kernel,
        grid=(hh // hg, nb),
        in_specs=[pl.BlockSpec((hg, tk, d), lambda h, i: (h, i, 0)),
                  pl.BlockSpec((hg, s_len, d), lambda h, i: (h, 0, 0), pipeline_mode=once),
                  pl.BlockSpec((hg, nb, rows, tk), lambda h, i: (h, 0, 0, 0), pipeline_mode=once),
                  pl.BlockSpec((hg, nb, tk), lambda h, i: (h, 0, i))],
        out_specs=pl.BlockSpec((tk, hg * d), lambda h, i: (i, h)),
        out_shape=jax.ShapeDtypeStruct((s_len, hh * d), F32),
        scratch_shapes=[pltpu.VMEM((hg, rows, tk), F32),
                        pltpu.VMEM((2, hg, _KV_PAIR * tk, tk), F32),
                        pltpu.VMEM((2, hg, _KV_PAIR * tk, tk), BF16)],
        compiler_params=_params("parallel", "arbitrary"),
        name="moba_attn",
    )(q16, k16, vt16, bias)


def _moba_sample_gate_kernel(q_ref, k_ref, cos_ref, sin_ref, km_ref, qr_ref, kr_ref, idx_ref, *, n_sel):
    q = _rope(q_ref[0], cos_ref[...], sin_ref[...])
    kr_ref[0] = _rope(k_ref[0], cos_ref[...], sin_ref[...])
    qr_ref[0] = q
    gate = jnp.sum(km_ref[0] * q, axis=-1)
    nb = gate.shape[0]
    blk = lax.broadcasted_iota(jnp.int32, gate.shape, 0)
    rows = []
    for _ in range(n_sel):
        top = jnp.max(gate, axis=0, keepdims=True)
        first = jnp.min(jnp.where(gate == top, blk, nb), axis=0, keepdims=True)
        rows.append(first)
        gate = jnp.where(blk == first, -jnp.inf, gate)
    idx_ref[0] = jnp.concatenate(rows, axis=0)


def _moba_sample_gate(q, k, cos, sin, kmean, n_sel):
    nbat, hh, d = q.shape
    nb = kmean.shape[1]
    row = lambda a: pl.BlockSpec((1,) + a.shape[1:], lambda b: (b,) + (0,) * (a.ndim - 1))
    full = lambda a: pl.BlockSpec(a.shape, lambda b: (0,) * a.ndim)
    return pl.pallas_call(
        functools.partial(_moba_sample_gate_kernel, n_sel=n_sel),
        grid=(nbat,),
        in_specs=[row(q), row(k), full(cos), full(sin), row(kmean)],
        out_specs=[row(q), row(k), pl.BlockSpec((1, n_sel, hh), lambda b: (b, 0, 0))],
        out_shape=[jax.ShapeDtypeStruct(q.shape, F32), jax.ShapeDtypeStruct(k.shape, F32),
                   jax.ShapeDtypeStruct((nbat, n_sel, hh), jnp.int32)],
        compiler_params=_params("parallel"),
        name="moba_sample_gate",
    )(q, k, cos, sin, kmean)


_SAMPLE_HEADS_PER_STEP = 2


def _moba_sample_attn_kernel(pt_ref, idx_ref, q_ref, kn_ref, vn_ref, *refs):
    del pt_ref, idx_ref
    o_ref = refs[-1]
    hps = q_ref.shape[1]
    n = (len(refs) - 1) // (2 * hps)
    k_refs, v_refs = refs[:n * hps], refs[n * hps:2 * n * hps]
    q8 = [jnp.broadcast_to(q_ref[0, h] * HEAD_DIM ** -0.5, (8, HEAD_DIM)) for h in range(hps)]
    scores = [[_dot_nt(q8[h].astype(BF16), kr[0, 0, 0].astype(BF16)) for kr in k_refs[h * n:(h + 1) * n]] for h in range(hps)]
    for h in range(hps):
        s_new = jnp.sum(q8[h] * kn_ref[0, h], axis=-1, keepdims=True)
        m = s_new
        for s in scores[h]:
            m = jnp.maximum(m, jnp.max(s, axis=-1, keepdims=True))
        e_new = jnp.exp(s_new - m)
        l = e_new
        acc = e_new * vn_ref[0, h]
        for s, vr in zip(scores[h], v_refs[h * n:(h + 1) * n]):
            e = jnp.exp(s - m)
            l = l + jnp.sum(e, axis=-1, keepdims=True)
            acc = acc + _dot(e.astype(BF16), vr[0, 0, 0].astype(BF16))
        o_ref[0, h] = (acc / l)[0:1, :]


def _moba_sample_attn(page_table, idx, q, k_new, v_new, pool_k, pool_v):
    nbat, hh, _, d = q.shape
    n_sel = idx.shape[1]
    idx = idx.reshape(nbat, n_sel * hh)
    ps = pool_k.shape[3]
    ppb = MOBA_BLOCK // PAGE_SIZE
    hps = _SAMPLE_HEADS_PER_STEP
    assert hh % hps == 0

    def page_map(b, hp, pt, ix, *, dh, j, r):
        h = hp * hps + dh
        return (0, pt[b, ix[b, j * hh + h] * ppb + r], h, 0, 0)

    page_specs = [pl.BlockSpec((1, 1, 1, ps, d), functools.partial(page_map, dh=dh, j=j, r=r))
                  for dh in range(hps) for j in range(n_sel) for r in range(ppb)]
    tok = pl.BlockSpec((1, hps, 1, d), lambda b, hp, pt, ix: (b, hp, 0, 0))
    gs = pltpu.PrefetchScalarGridSpec(
        num_scalar_prefetch=2, grid=(nbat, hh // hps),
        in_specs=[tok, tok, tok] + page_specs + page_specs,
        out_specs=tok)
    n_pg = len(page_specs)
    return pl.pallas_call(
        _moba_sample_attn_kernel, grid_spec=gs,
        out_shape=jax.ShapeDtypeStruct((nbat, hh, 1, d), F32),
        compiler_params=_params("parallel", "arbitrary"),
        name="moba_sample_attn",
    )(page_table, idx, q, k_new, v_new, *([pool_k] * n_pg), *([pool_v] * n_pg))


def _cross_sample_kernel(q_ref, mk_ref, mv_ref, o_ref):
    q = q_ref[0]
    outs = []
    for hd in range(CROSS_HEADS):
        sl = slice(hd * HEAD_DIM, (hd + 1) * HEAD_DIM)
        q16 = jnp.broadcast_to(q[:, sl], (8, HEAD_DIM)).astype(BF16)
        s = _dot_nt(q16, mk_ref[0, :, hd, :].astype(BF16)) * HEAD_DIM ** -0.5
        e = jnp.exp(s - jnp.max(s, axis=-1, keepdims=True))
        p = e / jnp.sum(e, axis=-1, keepdims=True)
        outs.append(_dot(p.astype(BF16), mv_ref[0, :, hd, :].astype(BF16))[0:1, :])
    o_ref[0] = jnp.concatenate(outs, axis=-1)


def _cross_sample(q, mem_k, mem_v):
    nbat = q.shape[0]
    row = lambda a: pl.BlockSpec((1,) + a.shape[1:], lambda b: (b,) + (0,) * (a.ndim - 1))
    return pl.pallas_call(
        _cross_sample_kernel,
        grid=(nbat,),
        in_specs=[row(q), row(mem_k), row(mem_v)],
        out_specs=row(q),
        out_shape=jax.ShapeDtypeStruct(q.shape, F32),
        compiler_params=_params("parallel"),
        name="cross_sample",
    )(q, mem_k, mem_v)


def _rope_tables(pos):
    half = ROPE_DIM // 2
    inv_freq = ROPE_THETA ** (-jnp.arange(0, ROPE_DIM, 2, dtype=F32) / ROPE_DIM)
    ang = pos.astype(F32)[:, None] * inv_freq[None, :]
    cos, sin = jnp.cos(ang), jnp.sin(ang)
    n = pos.shape[0]
    cos_t = jnp.concatenate([cos, cos, jnp.ones((n, HEAD_DIM - ROPE_DIM), F32)], axis=-1)
    sin_t = jnp.concatenate([-sin, sin, jnp.zeros((n, HEAD_DIM - ROPE_DIM), F32)], axis=-1)
    return cos_t, sin_t


def kernel(x_prompt, x_sample, cache_moba_k, cache_moba_v, page_table, state_gdn, state_conv, cache_mem_k, cache_mem_v, mem_prompt, w_ffn1_norm, w_ffn1_gu, w_ffn1_down, w_mix_norm, w_in, gdn_conv_w, gdn_a_log, gdn_dt_bias, gdn_out_norm, w_out, w_cross_norm, w_cross_q, w_cross_out, w_mem_norm, w_mem_kv, w_ffn2_norm, w_ffn2_gu, w_ffn2_down, w_final_norm):
    assert x_prompt.shape[0] == 1 and x_sample.shape[1] == 1 and w_in.shape[0] == 1
    s_len, d_model = x_prompt.shape[1], x_prompt.shape[2]
    nbat = x_sample.shape[0]
    past_len = page_table.shape[1] * PAGE_SIZE
    assert past_len % MOBA_BLOCK == 0

    row = lambda v: v.reshape(1, -1).astype(F32)
    g_ffn1, g_mix, g_cross, g_mem, g_ffn2 = (row(w[0]) for w in (w_ffn1_norm, w_mix_norm, w_cross_norm, w_mem_norm, w_ffn2_norm))
    g_final = row(w_final_norm)
    gu1, dn1 = w_ffn1_gu[0].astype(BF16), w_ffn1_down[0].astype(BF16)
    gu2, dn2 = w_ffn2_gu[0].astype(BF16), w_ffn2_down[0].astype(BF16)
    gdn_end = 4 * GDN_WIDTH
    ba_end = gdn_end + 2 * GDN_HEADS
    w_main = jnp.concatenate([w_in[0][:, :gdn_end], w_in[0][:, ba_end:]], axis=1).astype(BF16)
    w_ba = jnp.pad(w_in[0][:, gdn_end:ba_end], ((0, 0), (0, LANES - 2 * GDN_HEADS))).astype(BF16)
    wo_gdn, wo_moba = w_out[0][:GDN_WIDTH].astype(BF16), w_out[0][GDN_WIDTH:].astype(BF16)
    wq_c, wo_c = w_cross_q[0].astype(BF16), w_cross_out[0].astype(BF16)
    w_kv = w_mem_kv[0].astype(BF16)
    conv_w = gdn_conv_w[0]
    lane_pad = lambda v: jnp.pad(v.reshape(1, -1).astype(F32), ((0, 0), (GDN_HEADS, LANES - 2 * GDN_HEADS)))
    alog_row, dtb_row = lane_pad(gdn_a_log[0]), lane_pad(gdn_dt_bias[0])
    onorm = row(gdn_out_norm[0])
    moba_col0 = gdn_end

    xp = x_prompt[0]
    tm = _tile(s_len, 512)
    half = nbat // 2
    h, kmean_lo = _ffn(xp, g_ffn1, gu1, dn1, g_final, final_norm=False, tm=tm, tf=512,
                       paged=(page_table, cache_moba_k, 0, half))
    cols = _rms_matmul(h, g_mix, w_main, tm=_tile(s_len, 1024), tn=1024)
    ba = _rms_matmul(h, g_mix, w_ba, tm=_tile(s_len, 1024), tn=LANES)
    o_gdn, p_gdn_state, conv_tail = _gdn_prompt(cols, ba, conv_w, alog_row, dtb_row, onorm, chunks_per_step=2)
    cos_p, sin_p = _rope_tables(jnp.arange(s_len))
    q16, p_k, k16, p_v, vt16, bias = _moba_prep(cols, cos_p, sin_p, moba_col0)
    o_moba = _moba_attn(q16, k16, vt16, bias, hg=4)
    h = _proj_res(h, [(o_gdn, wo_gdn), (o_moba, wo_moba)], tm=tm)
    mem_kv = _rms_matmul(mem_prompt[0], g_mem, w_kv, tm=mem_prompt.shape[1], tn=2 * CROSS_WIDTH)
    mem_k, mem_v = mem_kv[:, :CROSS_WIDTH], mem_kv[:, CROSS_WIDTH:]
    h = _cross_prompt(h, g_cross, wq_c, mem_k.astype(BF16), mem_v.astype(BF16), wo_c, tm=tm)
    y_prompt, kmean_hi = _ffn(h, g_ffn2, gu2, dn2, g_final, final_norm=True, tm=tm, tf=512,
                              paged=(page_table, cache_moba_k, half, nbat - half))
    kmean = jnp.concatenate([kmean_lo, kmean_hi], axis=0)

    xs = x_sample[:, 0]
    hs = _ffn(xs, g_ffn1, gu1, dn1, g_final, final_norm=False, tm=nbat, tf=512)
    cols_s = _rms_matmul(hs, g_mix, w_main, tm=nbat, tn=1024)
    ba_s = _rms_matmul(hs, g_mix, w_ba, tm=nbat, tn=LANES)
    o_gdn_s, s_gdn_state, s_conv = _gdn_sample(
        cols_s[:, None, :GDN_CONV_CH], state_conv[0], cols_s[:, None, GDN_CONV_CH:gdn_end], ba_s[:, None, :],
        conv_w, alog_row, dtb_row, onorm, state_gdn[0])
    mq, mk, mv = (cols_s[:, moba_col0 + j * MOBA_WIDTH:moba_col0 + (j + 1) * MOBA_WIDTH].reshape(nbat, MOBA_HEADS, HEAD_DIM) for j in range(3))
    cos_s, sin_s = _rope_tables(jnp.full((1,), past_len))
    n_sel = min(MOBA_TOPK, past_len // MOBA_BLOCK)
    q_r, k_r, idx = _moba_sample_gate(mq, mk, cos_s, sin_s, kmean, n_sel)
    o_moba_s = _moba_sample_attn(page_table, idx, q_r[:, :, None, :], k_r[:, :, None, :], mv[:, :, None, :],
                                 cache_moba_k, cache_moba_v)
    hs = _proj_res(hs, [(o_gdn_s[:, 0], wo_gdn), (o_moba_s.reshape(nbat, MOBA_WIDTH), wo_moba)], tm=nbat)
    q_c = _rms_matmul(hs, g_cross, wq_c, tm=nbat, tn=CROSS_WIDTH)
    o_c = _cross_sample(q_c[:, None, :], cache_mem_k[0], cache_mem_v[0])
    hs = _proj_res(hs, [(o_c[:, 0], wo_c)], tm=nbat)
    y_sample = _ffn(hs, g_ffn2, gu2, dn2, g_final, final_norm=True, tm=nbat, tf=512)

    return (y_prompt[None], y_sample[:, None, :],
            p_k[None, None], p_v[None, None],
            p_gdn_state[None, None], conv_tail[8 - (CONV_WIDTH - 1):][None, None],
            mem_k.reshape(1, 1, -1, CROSS_HEADS, HEAD_DIM), mem_v.reshape(1, 1, -1, CROSS_HEADS, HEAD_DIM),
            k_r[None, :, :, None, :], mv[None, :, :, None, :],
            s_gdn_state[None], s_conv[None])
```

```python
import functools

import jax
import jax.numpy as jnp
from jax import lax
from jax.experimental import pallas as pl
from jax.experimental.pallas import tpu as pltpu

F32 = jnp.float32
BF16 = jnp.bfloat16
HIGHEST = lax.Precision.HIGHEST

HEAD_DIM = 128
GDN_HEADS = 8
MOBA_HEADS = 8
GDN_WIDTH = GDN_HEADS * HEAD_DIM
MOBA_WIDTH = MOBA_HEADS * HEAD_DIM
CONV_WIDTH = 4
GDN_CONV_CH = 3 * GDN_WIDTH
GDN_CHUNK = 64
MOBA_BLOCK = 256
MOBA_TOPK = 3
PAGE_SIZE = 128
ROPE_DIM = HEAD_DIM // 4
ROPE_THETA = 500000.0
CROSS_HEADS = 4
CROSS_WIDTH = CROSS_HEADS * HEAD_DIM
NORM_EPS = 1e-6

LANES = 128
MASK_VALUE = -1e30
LOG2_E = 1.4426950408889634
VT_ROWS = HEAD_DIM + 16
VMEM_LIMIT_BYTES = 56 * 1024 * 1024

_NT = (((1,), (1,)), ((), ()))
_TN = (((0,), (0,)), ((), ()))


def _params(*sem):
    return pltpu.CompilerParams(dimension_semantics=sem, vmem_limit_bytes=VMEM_LIMIT_BYTES)


def _rms(x, g):
    return x * lax.rsqrt(jnp.mean(x * x, axis=-1, keepdims=True) + NORM_EPS) * g


def _silu(x):
    return x * jax.nn.sigmoid(x)


def _dot(a, b):
    return jnp.dot(a, b, preferred_element_type=F32)


def _dot_nt(a, b):
    return lax.dot_general(a, b, _NT, preferred_element_type=F32)


def _dot_f32(a, b):
    return jnp.dot(a, b, precision=HIGHEST, preferred_element_type=F32)


def _col(x, c, width):
    return jnp.broadcast_to(x[:, c:c + 1], (x.shape[0], width))


def _tile(m, pref):
    return pref if m % pref == 0 else m


def _rms_matmul_kernel(x_ref, g_ref, *refs, starts):
    w_refs, o_ref, xn_ref = refs[:-2], refs[-2], refs[-1]
    j = pl.program_id(1)

    @pl.when(j == 0)
    def _():
        xn_ref[...] = _rms(x_ref[...], g_ref[...]).astype(BF16)

    for w_ref, lo, hi in zip(w_refs, starts[:-1], starts[1:]):
        @pl.when((j >= lo) & (j < hi))
        def _():
            o_ref[...] = _dot(xn_ref[...], w_ref[...])


def _rms_matmul(x, g, ws, *, tm, tn):
    m, d = x.shape
    starts = [0]
    for w in ws:
        assert w.shape[1] % tn == 0
        starts.append(starts[-1] + w.shape[1] // tn)
    w_specs = [pl.BlockSpec((d, tn), functools.partial(lambda i, j, lo, hi: (0, jnp.clip(j, lo, hi - 1) - lo), lo=lo, hi=hi))
               for lo, hi in zip(starts[:-1], starts[1:])]
    return pl.pallas_call(
        functools.partial(_rms_matmul_kernel, starts=tuple(starts)),
        grid=(m // tm, starts[-1]),
        in_specs=[pl.BlockSpec((tm, d), lambda i, j: (i, 0)),
                  pl.BlockSpec((1, d), lambda i, j: (0, 0))] + w_specs,
        out_specs=pl.BlockSpec((tm, tn), lambda i, j: (i, j)),
        out_shape=jax.ShapeDtypeStruct((m, starts[-1] * tn), F32),
        scratch_shapes=[pltpu.VMEM((tm, d), BF16)],
        compiler_params=_params("parallel", "arbitrary"),
        name="rms_matmul",
    )(x, g, *ws)


_PAGES_PER_STEP = 8


def _block_key_means(page_refs, o_ref):
    ppb = MOBA_BLOCK // PAGE_SIZE
    for blk in range(len(page_refs) // ppb):
        tot = jnp.sum(page_refs[blk * ppb][0, 0], axis=1)
        for r in range(1, ppb):
            tot = tot + jnp.sum(page_refs[blk * ppb + r][0, 0], axis=1)
        o_ref[0, blk] = tot * (1.0 / MOBA_BLOCK)


def _ffn_kernel(*refs, final_norm, n_pages):
    if n_pages:
        refs = refs[1:]
    x_ref, g_ref, wg_ref, wu_ref, wd_ref, fg_ref = refs[:6]
    page_refs = refs[6:6 + n_pages]
    o_ref, xn_ref = refs[6 + n_pages], refs[-1]
    j = pl.program_id(1)

    @pl.when(j == 0)
    def _():
        xn_ref[...] = _rms(x_ref[...], g_ref[...]).astype(BF16)
        o_ref[...] = jnp.zeros_like(o_ref)

    xn = xn_ref[...]
    act = _silu(_dot(xn, wg_ref[...])) * _dot(xn, wu_ref[...])
    o_ref[...] += _dot(act.astype(BF16), wd_ref[...])
    if n_pages:
        _block_key_means(page_refs, refs[7 + n_pages])

    @pl.when(j == pl.num_programs(1) - 1)
    def _():
        h = x_ref[...] + 0.5 * o_ref[...]
        if final_norm:
            h = _rms(h, fg_ref[...])
        o_ref[...] = h


def _kmean_slots(page_table_rows, n_pages):
    return page_table_rows * (n_pages // _PAGES_PER_STEP)


def _ffn(x, g, w_gu, w_down, fg, *, final_norm, tm, tf, paged=None):
    m, d = x.shape
    f = w_down.shape[0]
    nf = f // tf
    grid = (m // tm, nf)
    in_specs = [pl.BlockSpec((tm, d), lambda i, j, *_: (i, 0)),
                pl.BlockSpec((1, d), lambda i, j, *_: (0, 0)),
                pl.BlockSpec((d, tf), lambda i, j, *_: (0, j)),
                pl.BlockSpec((d, tf), lambda i, j, *_: (0, nf + j)),
                pl.BlockSpec((tf, d), lambda i, j, *_: (j, 0)),
                pl.BlockSpec((1, d), lambda i, j, *_: (0, 0))]
    out_spec = pl.BlockSpec((tm, d), lambda i, j, *_: (i, 0))
    out_shape = jax.ShapeDtypeStruct((m, d), F32)
    scratch = [pltpu.VMEM((tm, d), BF16)]
    if paged is None:
        return pl.pallas_call(
            functools.partial(_ffn_kernel, final_norm=final_norm, n_pages=0),
            grid=grid, in_specs=in_specs, out_specs=out_spec, out_shape=out_shape, scratch_shapes=scratch,
            compiler_params=_params("parallel", "arbitrary"), name="ffn",
        )(x, g, w_gu, w_gu, w_down, fg)

    page_table, pool_k, row0, n_rows = paged
    n_pages = page_table.shape[1]
    _, _, hh, ps, hd = pool_k.shape
    groups = n_pages // _PAGES_PER_STEP
    n_slots = _kmean_slots(n_rows, n_pages)
    assert n_pages % _PAGES_PER_STEP == 0 and n_slots <= grid[0] * grid[1]
    ppb = MOBA_BLOCK // PAGE_SIZE

    def slot(i, j):
        t = jnp.minimum(i * nf + j, n_slots - 1)
        return lax.div(t, groups), lax.rem(t, groups)

    def page_map(i, j, pt, *, r):
        b, c = slot(i, j)
        return (0, pt[row0 + b, c * _PAGES_PER_STEP + r], 0, 0, 0)

    def kmean_map(i, j, pt):
        b, c = slot(i, j)
        return (b, c, 0, 0)

    page_specs = [pl.BlockSpec((1, 1, hh, ps, hd), functools.partial(page_map, r=r)) for r in range(_PAGES_PER_STEP)]
    gs = pltpu.PrefetchScalarGridSpec(
        num_scalar_prefetch=1, grid=grid, in_specs=in_specs + page_specs,
        out_specs=[out_spec, pl.BlockSpec((1, _PAGES_PER_STEP // ppb, hh, hd), kmean_map)],
        scratch_shapes=scratch)
    return pl.pallas_call(
        functools.partial(_ffn_kernel, final_norm=final_norm, n_pages=_PAGES_PER_STEP),
        grid_spec=gs,
        out_shape=[out_shape, jax.ShapeDtypeStruct((n_rows, n_pages // ppb, hh, hd), F32)],
        compiler_params=_params("arbitrary", "arbitrary"), name="ffn_kmean",
    )(page_table, x, g, w_gu, w_gu, w_down, fg, *([pool_k] * _PAGES_PER_STEP))


def _proj_res_kernel(*refs):
    res_ref, o_ref = refs[0], refs[-1]
    pairs = refs[1:-1]
    acc = res_ref[...]
    for a_ref, w_ref in zip(pairs[0::2], pairs[1::2]):
        acc = acc + _dot(a_ref[...].astype(BF16), w_ref[...])
    o_ref[...] = acc


def _proj_res(res, pairs, *, tm):
    m, d = res.shape
    in_specs = [pl.BlockSpec((tm, d), lambda i: (i, 0))]
    args = [res]
    for a, w in pairs:
        in_specs.append(pl.BlockSpec((tm, a.shape[1]), lambda i: (i, 0)))
        in_specs.append(pl.BlockSpec(w.shape, lambda i: (0, 0)))
        args += [a, w]
    return pl.pallas_call(
        _proj_res_kernel,
        grid=(m // tm,),
        in_specs=in_specs,
        out_specs=pl.BlockSpec((tm, d), lambda i: (i, 0)),
        out_shape=jax.ShapeDtypeStruct((m, d), F32),
        compiler_params=_params("parallel"),
        name="proj_res",
    )(*args)


def _cross_kernel(res_ref, a_ref, b_ref, wa_ref, wb_ref, g_ref, wq_ref, mk_ref, mv_ref, wo_ref, o_ref):
    h = res_ref[...] + _dot(a_ref[...].astype(BF16), wa_ref[...]) + _dot(b_ref[...].astype(BF16), wb_ref[...])
    q = _dot(_rms(h, g_ref[...]).astype(BF16), wq_ref[...])
    outs = []
    for hd in range(CROSS_HEADS):
        sl = slice(hd * HEAD_DIM, (hd + 1) * HEAD_DIM)
        s = _dot_nt(q[:, sl].astype(BF16), mk_ref[:, sl]) * HEAD_DIM ** -0.5
        e = jnp.exp(s - jnp.max(s, axis=-1, keepdims=True))
        p = e / jnp.sum(e, axis=-1, keepdims=True)
        outs.append(_dot(p.astype(BF16), mv_ref[:, sl]).astype(BF16))
    o_ref[...] = h + _dot(jnp.concatenate(outs, axis=-1), wo_ref[...])


def _cross_prompt(res, a, b, wa, wb, g, wq, mem_k, mem_v, wo, *, tm):
    m, d = res.shape
    rows = lambda x: pl.BlockSpec((tm, x.shape[1]), lambda i: (i, 0))
    full = lambda x: pl.BlockSpec(x.shape, lambda i: (0, 0), pipeline_mode=pl.Buffered(1))
    return pl.pallas_call(
        _cross_kernel,
        grid=(m // tm,),
        in_specs=[rows(res), rows(a), rows(b), full(wa), full(wb), full(g), full(wq), full(mem_k), full(mem_v), full(wo)],
        out_specs=pl.BlockSpec((tm, d), lambda i: (i, 0)),
        out_shape=jax.ShapeDtypeStruct((m, d), F32),
        compiler_params=_params("parallel"),
        name="cross_prompt",
    )(res, a, b, wa, wb, g, wq, mem_k, mem_v, wo)


def _split(x):
    hi = x.astype(BF16)
    return hi, (x - hi.astype(F32)).astype(BF16)


def _dot_split(a, b):
    return _dot(a[0], b[0]) + _dot(a[0], b[1]) + _dot(a[1], b[0])


def _gdn_prompt_kernel(qkv_ref, z_ref, ba_ref, cw_ref, alog_ref, dtb_ref, onorm_ref,
                       o_ref, sfin_ref, convn_ref, xp_ref, y_ref, s_ref):
    c = GDN_CHUNK
    rows = qkv_ref.shape[0]
    nck = rows // c
    i = pl.program_id(0)

    @pl.when(i == 0)
    def _():
        xp_ref[0:8, :] = jnp.zeros((8, GDN_CONV_CH), F32)
        s_ref[...] = jnp.zeros_like(s_ref)

    xp_ref[8:8 + rows, :] = qkv_ref[...]
    y = cw_ref[0:1, :] * xp_ref[5:5 + rows, :]
    for j in range(1, CONV_WIDTH):
        y = y + cw_ref[j:j + 1, :] * xp_ref[5 + j:5 + j + rows, :]
    y_ref[...] = _silu(y)
    tail = xp_ref[rows:rows + 8, :]
    xp_ref[0:8, :] = tail
    convn_ref[...] = tail

    ba = ba_ref[...]
    beta = jax.nn.sigmoid(ba)
    g = -jnp.exp(alog_ref[...]) * jax.nn.softplus(ba + dtb_ref[...])
    row = lax.broadcasted_iota(jnp.int32, (c, c), 0)
    colv = lax.broadcasted_iota(jnp.int32, (c, c), 1)
    causal = row >= colv
    strict = row > colv
    eye = (row == colv).astype(F32)
    tri = causal.astype(F32)
    heads = range(GDN_HEADS)

    work = []
    for ck in range(nck):
        r0 = ck * c
        gc = _dot_f32(tri, g[r0:r0 + c])
        gc_t = gc.T
        eg = jnp.exp(gc)
        g_last = gc[c - 1:c, :]
        eg_last = jnp.exp(g_last)
        k_decay = jnp.exp(g_last - gc)
        for h in heads:
            gh = GDN_HEADS + h
            qh = y_ref[r0:r0 + c, h * HEAD_DIM:(h + 1) * HEAD_DIM]
            kh = y_ref[r0:r0 + c, GDN_WIDTH + h * HEAD_DIM:GDN_WIDTH + (h + 1) * HEAD_DIM]
            vh = y_ref[r0:r0 + c, 2 * GDN_WIDTH + h * HEAD_DIM:2 * GDN_WIDTH + (h + 1) * HEAD_DIM]
            qn = qh * lax.rsqrt(jnp.sum(qh * qh, axis=-1, keepdims=True) + NORM_EPS) * HEAD_DIM ** -0.5
            kn = kh * lax.rsqrt(jnp.sum(kh * kh, axis=-1, keepdims=True) + NORM_EPS)
            b_col = _col(beta[r0:r0 + c], h, HEAD_DIM)
            eg_col = _col(eg, gh, HEAD_DIM)
            diff = _col(gc, gh, c) - jnp.broadcast_to(gc_t[gh:gh + 1, :], (c, c))
            decay = jnp.exp(jnp.where(causal, diff, -jnp.inf))
            kb = kn * b_col
            kn16 = kn.astype(BF16)
            qn16 = qn.astype(BF16)
            work.append(dict(
                a=_dot_nt(kb.astype(BF16), kn16) * jnp.where(strict, decay, 0.0),
                attn=(_dot_nt(qn16, kn16) * decay).astype(BF16),
                vb=(vh * b_col).astype(BF16),
                kbg=(kb * eg_col).astype(BF16),
                qg=(qn * eg_col).astype(BF16),
                kd=(kn * _col(k_decay, gh, HEAD_DIM)).astype(BF16),
                eg_last=eg_last[:, gh:gh + 1]))

    ps = [_split(-wk["a"]) for wk in work]
    ts = [eye - wk["a"] for wk in work]
    for _ in range(5):
        ps = [_split(_dot_split(p, p)) for p in ps]
        ts = [t + _dot_split(_split(t), p) for t, p in zip(ts, ps)]
    for wk, t in zip(work, ts):
        t16 = t.astype(BF16)
        wk["u"] = _dot(t16, wk["vb"])
        wk["w"] = _dot(t16, wk["kbg"]).astype(BF16)

    for ck in range(nck):
        r0 = ck * c
        wks = work[ck * GDN_HEADS:(ck + 1) * GDN_HEADS]
        states = [s_ref[h] for h in heads]
        s16 = [s.astype(BF16) for s in states]
        v_new = [(wk["u"] - _dot(wk["w"], s16[h])).astype(BF16) for h, wk in zip(heads, wks)]
        outs = [_dot(wk["qg"], s16[h]) + _dot(wk["attn"], v_new[h]) for h, wk in zip(heads, wks)]
        for h, wk in zip(heads, wks):
            s_ref[h] = states[h] * wk["eg_last"] + lax.dot_general(wk["kd"], v_new[h], _TN, preferred_element_type=F32)
        for h, o in zip(heads, outs):
            sl = slice(h * HEAD_DIM, (h + 1) * HEAD_DIM)
            o = o * lax.rsqrt(jnp.mean(o * o, axis=-1, keepdims=True) + NORM_EPS) * onorm_ref[...]
            o_ref[r0:r0 + c, sl] = o * _silu(z_ref[r0:r0 + c, sl])

    @pl.when(i == pl.num_programs(0) - 1)
    def _():
        sfin_ref[...] = s_ref[...]


def _gdn_prompt(cols, ba, conv_w, alog_row, dtb_row, onorm, *, chunks_per_step):
    s_len = cols.shape[0]
    c = GDN_CHUNK * chunks_per_step
    assert s_len % c == 0
    full = lambda a: pl.BlockSpec(a.shape, lambda i: (0,) * a.ndim)
    return pl.pallas_call(
        _gdn_prompt_kernel,
        grid=(s_len // c,),
        in_specs=[pl.BlockSpec((c, GDN_CONV_CH), lambda i: (i, 0)),
                  pl.BlockSpec((c, GDN_WIDTH), lambda i: (i, GDN_CONV_CH // GDN_WIDTH)),
                  pl.BlockSpec((c, LANES), lambda i: (i, 0)),
                  full(conv_w), full(alog_row), full(dtb_row), full(onorm)],
        out_specs=[pl.BlockSpec((c, GDN_WIDTH), lambda i: (i, 0)),
                   pl.BlockSpec((GDN_HEADS, HEAD_DIM, HEAD_DIM), lambda i: (0, 0, 0)),
                   pl.BlockSpec((8, GDN_CONV_CH), lambda i: (0, 0))],
        out_shape=[jax.ShapeDtypeStruct((s_len, GDN_WIDTH), F32),
                   jax.ShapeDtypeStruct((GDN_HEADS, HEAD_DIM, HEAD_DIM), F32),
                   jax.ShapeDtypeStruct((8, GDN_CONV_CH), F32)],
        scratch_shapes=[pltpu.VMEM((c + 8, GDN_CONV_CH), F32),
                        pltpu.VMEM((c, GDN_CONV_CH), F32),
                        pltpu.VMEM((GDN_HEADS, HEAD_DIM, HEAD_DIM), F32)],
        compiler_params=_params("arbitrary"),
        name="gdn_prompt",
    )(cols, cols, ba, conv_w, alog_row, dtb_row, onorm)


def _gdn_sample_kernel(x_ref, cs_ref, z_ref, ba_ref, cw_ref, alog_ref, dtb_ref, onorm_ref, s0_ref,
                       o_ref, s_ref, convn_ref):
    x = x_ref[0]
    cs = cs_ref[0]
    y = cw_ref[CONV_WIDTH - 1:CONV_WIDTH, :] * x
    for j in range(CONV_WIDTH - 1):
        y = y + cw_ref[j:j + 1, :] * cs[j:j + 1, :]
    y = _silu(y)
    convn_ref[0] = jnp.concatenate([cs[1:CONV_WIDTH - 1, :], x], axis=0)

    ba = ba_ref[0]
    beta = jax.nn.sigmoid(ba)
    eg = jnp.exp(-jnp.exp(alog_ref[...]) * jax.nn.softplus(ba + dtb_ref[...]))
    z = z_ref[0]

    def head_rows(base):
        return jnp.concatenate([y[:, base + h * HEAD_DIM:base + (h + 1) * HEAD_DIM] for h in range(GDN_HEADS)], axis=0)

    q8 = head_rows(0)
    k8 = head_rows(GDN_WIDTH)
    v8 = head_rows(2 * GDN_WIDTH)
    q8 = q8 * lax.rsqrt(jnp.sum(q8 * q8, axis=-1, keepdims=True) + NORM_EPS) * HEAD_DIM ** -0.5
    k8 = k8 * lax.rsqrt(jnp.sum(k8 * k8, axis=-1, keepdims=True) + NORM_EPS)
    qk8 = jnp.sum(q8 * k8, axis=-1, keepdims=True)
    q_t = q8.T
    k_t = k8.T
    outs = []
    for h in range(GDN_HEADS):
        s = s0_ref[0, h]
        k_col = _col(k_t, h, HEAD_DIM)
        q_col = _col(q_t, h, HEAD_DIM)
        e = eg[:, GDN_HEADS + h:GDN_HEADS + h + 1]
        k_s = jnp.sum(k_col * s, axis=0, keepdims=True)
        q_s = jnp.sum(q_col * s, axis=0, keepdims=True)
        v_new = beta[:, h:h + 1] * (v8[h:h + 1, :] - e * k_s)
        o = e * q_s + qk8[h:h + 1, :] * v_new
        s_ref[0, h] = s * e + k_col * v_new
        o = o * lax.rsqrt(jnp.mean(o * o, axis=-1, keepdims=True) + NORM_EPS) * onorm_ref[...]
        outs.append(o * _silu(z[:, h * HEAD_DIM:(h + 1) * HEAD_DIM]))
    o_ref[0] = jnp.concatenate(outs, axis=-1)


def _gdn_sample(x, conv_state, z, ba, conv_w, alog_row, dtb_row, onorm, s0):
    nb = x.shape[0]
    full = lambda a: pl.BlockSpec(a.shape, lambda b: (0,) * a.ndim)
    row = lambda a: pl.BlockSpec((1,) + a.shape[1:], lambda b: (b,) + (0,) * (a.ndim - 1))
    return pl.pallas_call(
        _gdn_sample_kernel,
        grid=(nb,),
        in_specs=[row(x), row(conv_state), row(z), row(ba), full(conv_w), full(alog_row), full(dtb_row),
                  full(onorm), row(s0)],
        out_specs=[pl.BlockSpec((1, 1, GDN_WIDTH), lambda b: (b, 0, 0)), row(s0), row(conv_state)],
        out_shape=[jax.ShapeDtypeStruct((nb, 1, GDN_WIDTH), F32),
                   jax.ShapeDtypeStruct(s0.shape, F32),
                   jax.ShapeDtypeStruct(conv_state.shape, F32)],
        compiler_params=_params("parallel"),
        name="gdn_sample",
    )(x, conv_state, z, ba, conv_w, alog_row, dtb_row, onorm, s0)


def _rope(x, cos, sin_signed):
    lane = lax.broadcasted_iota(jnp.int32, x.shape, x.ndim - 1)
    half = ROPE_DIM // 2
    partner = jnp.where(lane < half, pltpu.roll(x, LANES - half, x.ndim - 1), pltpu.roll(x, half, x.ndim - 1))
    return x * cos + partner * sin_signed


def _moba_prep_kernel(q_ref, k_ref, v_ref, cos_in_ref, sin_in_ref, cos_blk_ref, sin_blk_ref,
                      q16_ref, kf_ref, k16_ref, vf_ref, vt16_ref, bias_ref, kmean_ref, *, n_sel):
    i = pl.program_id(0)
    nb = kmean_ref.shape[1]
    t = q_ref.shape[0]

    @pl.when(i == 0)
    def _():
        kmean_ref[...] = jnp.zeros_like(kmean_ref)

    cb, sb = cos_blk_ref[pl.ds(i, 1), :], sin_blk_ref[pl.ds(i, 1), :]
    cos = cb * cos_in_ref[...] - sb * sin_in_ref[...]
    sin = sb * cos_in_ref[...] + cb * sin_in_ref[...]
    blk = lax.broadcasted_iota(jnp.int32, (nb, t), 0)
    past = blk < i
    for h in range(MOBA_HEADS):
        sl = slice(h * HEAD_DIM, (h + 1) * HEAD_DIM)
        q = _rope(q_ref[:, sl], cos, sin)
        k = _rope(k_ref[:, sl], cos, sin)
        v = v_ref[:, sl]
        q16_ref[h] = (q * (HEAD_DIM ** -0.5 * LOG2_E)).astype(BF16)
        kf_ref[h] = k
        k16_ref[h] = k.astype(BF16)
        vf_ref[h] = v
        vt16_ref[h, 0, 0:HEAD_DIM, :] = v.T.astype(BF16)
        vt16_ref[h, 0, HEAD_DIM:VT_ROWS, :] = (lax.broadcasted_iota(jnp.int32, (VT_ROWS - HEAD_DIM, t), 0) == 0).astype(BF16)
        gate = lax.dot_general(kmean_ref[h], q, _NT, precision=HIGHEST, preferred_element_type=F32)
        gate = jnp.where(past, gate, -jnp.inf)
        chosen = jnp.zeros((nb, t), F32)
        for _ in range(n_sel):
            top = jnp.max(gate, axis=0, keepdims=True)
            first = jnp.min(jnp.where(gate == top, blk, nb), axis=0, keepdims=True)
            pick = (blk == first) & (top > -jnp.inf)
            chosen = jnp.where(pick, 1.0, chosen)
            gate = jnp.where(pick, -jnp.inf, gate)
        bias_ref[h] = jnp.where(chosen > 0.0, 0.0, MASK_VALUE)
        kmean_ref[h, pl.ds(i, 1), :] = jnp.mean(k, axis=0, keepdims=True)


def _moba_prep(cols, col0):
    s_len = cols.shape[0]
    t = MOBA_BLOCK
    assert s_len % t == 0
    nb = s_len // t
    hh = MOBA_HEADS
    cb = col0 // MOBA_WIDTH
    n_sel = min(MOBA_TOPK, nb - 1)
    cos_in, sin_in = _rope_tables(jnp.arange(t))
    cos_blk, sin_blk = _rope_tables(jnp.arange(nb) * t)
    full = lambda a: pl.BlockSpec(a.shape, lambda i: (0, 0))
    return pl.pallas_call(
        functools.partial(_moba_prep_kernel, n_sel=n_sel),
        grid=(nb,),
        in_specs=[pl.BlockSpec((t, MOBA_WIDTH), lambda i: (i, cb)),
                  pl.BlockSpec((t, MOBA_WIDTH), lambda i: (i, cb + 1)),
                  pl.BlockSpec((t, MOBA_WIDTH), lambda i: (i, cb + 2)),
                  full(cos_in), full(sin_in), full(cos_blk), full(sin_blk)],
        out_specs=[pl.BlockSpec((hh, t, HEAD_DIM), lambda i: (0, i, 0)),
                   pl.BlockSpec((hh, t, HEAD_DIM), lambda i: (0, i, 0)),
                   pl.BlockSpec((hh, t, HEAD_DIM), lambda i: (0, i, 0)),
                   pl.BlockSpec((hh, t, HEAD_DIM), lambda i: (0, i, 0)),
                   pl.BlockSpec((hh, 1, VT_ROWS, t), lambda i: (0, i, 0, 0)),
                   pl.BlockSpec((hh, nb, t), lambda i: (0, 0, i))],
        out_shape=[jax.ShapeDtypeStruct((hh, s_len, HEAD_DIM), BF16),
                   jax.ShapeDtypeStruct((hh, s_len, HEAD_DIM), F32),
                   jax.ShapeDtypeStruct((hh, s_len, HEAD_DIM), BF16),
                   jax.ShapeDtypeStruct((hh, s_len, HEAD_DIM), F32),
                   jax.ShapeDtypeStruct((hh, nb, VT_ROWS, t), BF16),
                   jax.ShapeDtypeStruct((hh, nb, s_len), F32)],
        scratch_shapes=[pltpu.VMEM((hh, nb, HEAD_DIM), F32)],
        compiler_params=_params("arbitrary"),
        name="moba_prep",
    )(cols, cols, cols, cos_in, sin_in, cos_blk, sin_blk)


_KV_PAIR = 2


def _moba_attn_kernel(q_ref, k_ref, vt_ref, bias_ref, o_ref, acc_ref, s_ref, p_ref):
    hg, tq, _ = q_ref.shape
    tk = vt_ref.shape[3]
    own = pl.program_id(1)
    diag = own // _KV_PAIR
    kpos = lax.broadcasted_iota(jnp.int32, (tk, tq), 0)
    qpos = lax.broadcasted_iota(jnp.int32, (tk, tq), 1)
    causal_bias = jnp.where(kpos <= qpos, 0.0, MASK_VALUE)

    def scores(j, slot, diagonal=False):
        g = diag if diagonal else jnp.minimum(j - 1, diag)
        real = j <= diag
        off = pl.multiple_of(g * (_KV_PAIR * tk), _KV_PAIR * tk)
        tops = []
        for h in range(hg):
            s = _dot_nt(k_ref[h, pl.ds(off, _KV_PAIR * tk), :], q_ref[h])
            top = None
            for r in range(_KV_PAIR):
                kb = g * _KV_PAIR + r
                if diagonal:
                    row = jnp.broadcast_to(bias_ref[h, pl.ds(kb, 1), :], (tk, tq))
                    bias = jnp.where(kb == own, causal_bias, row)
                else:
                    bias = jnp.where(real, bias_ref[h, pl.ds(kb, 1), :], MASK_VALUE)
                sb = s[r * tk:(r + 1) * tk] + bias
                s_ref[slot, h, r * tk:(r + 1) * tk, :] = sb
                mx = jnp.max(sb, axis=0, keepdims=True)
                top = mx if top is None else jnp.maximum(top, mx)
            tops.append(top)
        return tuple(tops)

    def softmax(slot, ms, tops):
        new, alphas = [], []
        for h in range(hg):
            m_new = jnp.maximum(ms[h], tops[h])
            p_ref[slot, h] = jnp.exp2(s_ref[slot, h] - m_new).astype(BF16)
            new.append(m_new)
            alphas.append(jnp.exp2(ms[h] - m_new))
        return tuple(new), tuple(alphas)

    def accumulate(j, slot, alphas):
        g = jnp.where(j == 0, diag, jnp.minimum(j - 1, diag))
        for h in range(hg):
            pv = _dot(vt_ref[h, g * _KV_PAIR], p_ref[slot, h, 0:tk, :])
            for r in range(1, _KV_PAIR):
                pv = pv + _dot(vt_ref[h, g * _KV_PAIR + r], p_ref[slot, h, r * tk:(r + 1) * tk, :])
            acc_ref[h] = acc_ref[h] * alphas[h] + pv

    acc_ref[...] = jnp.zeros_like(acc_ref)
    m0 = jnp.full((1, tq), MASK_VALUE, F32)
    tops0 = scores(0, 0, diagonal=True)
    tops1 = scores(1, 1)
    carry = softmax(0, (m0,) * hg, tops0) + (tops1,)

    def step(j, cur, carry):
        ms, alphas, tops = carry
        next_tops = scores(j + 1, 1 - cur)
        new = softmax(cur, ms, tops)
        accumulate(j - 1, 1 - cur, alphas)
        return new + (next_tops,)

    def two_steps(t, carry):
        return step(2 * t + 2, 0, step(2 * t + 1, 1, carry))

    half = (diag + 1) // 2
    _, alphas, _ = lax.fori_loop(0, half, two_steps, carry)
    accumulate(2 * half, 0, alphas)
    for h in range(hg):
        acc = acc_ref[h]
        o_ref[:, h * HEAD_DIM:(h + 1) * HEAD_DIM] = (acc[0:HEAD_DIM] / acc[HEAD_DIM:HEAD_DIM + 1]).T


def _moba_attn(q16, k16, vt16, bias, *, hg):
    hh, s_len, d = q16.shape
    nb, rows, tk = vt16.shape[1:]
    assert nb % _KV_PAIR == 0 and hh % hg == 0
    once = pl.Buffered(1)
    return pl.pallas_call(
        _moba_attn_kernel,
        grid=(hh // hg, nb),
        in_specs=[pl.BlockSpec((hg, tk, d), lambda h, i: (h, i, 0)),
                  pl.BlockSpec((hg, s_len, d), lambda h, i: (h, 0, 0), pipeline_mode=once),
                  pl.BlockSpec((hg, nb, rows, tk), lambda h, i: (h, 0, 0, 0), pipeline_mode=once),
                  pl.BlockSpec((hg, nb, tk), lambda h, i: (h, 0, i))],
        out_specs=pl.BlockSpec((tk, hg * d), lambda h, i: (i, h)),
        out_shape=jax.ShapeDtypeStruct((s_len, hh * d), F32),
        scratch_shapes=[pltpu.VMEM((hg, rows, tk), F32),
                        pltpu.VMEM((2, hg, _KV_PAIR * tk, tk), F32),
                        pltpu.VMEM((2, hg, _KV_PAIR * tk, tk), BF16)],
        compiler_params=_params("parallel", "arbitrary"),
        name="moba_attn",
    )(q16, k16, vt16, bias)


def _moba_sample_gate_kernel(q_ref, k_ref, cos_ref, sin_ref, km_ref, qr_ref, kr_ref, idx_ref, *, n_sel):
    q = _rope(q_ref[0], cos_ref[...], sin_ref[...])
    kr_ref[0] = _rope(k_ref[0], cos_ref[...], sin_ref[...])
    qr_ref[0] = q
    gate = jnp.sum(km_ref[0] * q, axis=-1)
    nb = gate.shape[0]
    blk = lax.broadcasted_iota(jnp.int32, gate.shape, 0)
    rows = []
    for _ in range(n_sel):
        top = jnp.max(gate, axis=0, keepdims=True)
        first = jnp.min(jnp.where(gate == top, blk, nb), axis=0, keepdims=True)
        rows.append(first)
        gate = jnp.where(blk == first, -jnp.inf, gate)
    idx_ref[0] = jnp.concatenate(rows, axis=0)


def _moba_sample_gate(q, k, cos, sin, kmean, n_sel):
    nbat, hh, d = q.shape
    nb = kmean.shape[1]
    row = lambda a: pl.BlockSpec((1,) + a.shape[1:], lambda b: (b,) + (0,) * (a.ndim - 1))
    full = lambda a: pl.BlockSpec(a.shape, lambda b: (0,) * a.ndim)
    return pl.pallas_call(
        functools.partial(_moba_sample_gate_kernel, n_sel=n_sel),
        grid=(nbat,),
        in_specs=[row(q), row(k), full(cos), full(sin), row(kmean)],
        out_specs=[row(q), row(k), pl.BlockSpec((1, n_sel, hh), lambda b: (b, 0, 0))],
        out_shape=[jax.ShapeDtypeStruct(q.shape, F32), jax.ShapeDtypeStruct(k.shape, F32),
                   jax.ShapeDtypeStruct((nbat, n_sel, hh), jnp.int32)],
        compiler_params=_params("parallel"),
        name="moba_sample_gate",
    )(q, k, cos, sin, kmean)


_SAMPLE_HEADS_PER_STEP = 2


def _moba_sample_attn_kernel(pt_ref, idx_ref, q_ref, kn_ref, vn_ref, *refs):
    del pt_ref, idx_ref
    o_ref = refs[-1]
    hps = q_ref.shape[1]
    n = (len(refs) - 1) // (2 * hps)
    k_refs, v_refs = refs[:n * hps], refs[n * hps:2 * n * hps]
    q8 = [jnp.broadcast_to(q_ref[0, h] * HEAD_DIM ** -0.5, (8, HEAD_DIM)) for h in range(hps)]
    scores = [[_dot_nt(q8[h].astype(BF16), kr[0, 0, 0].astype(BF16)) for kr in k_refs[h * n:(h + 1) * n]] for h in range(hps)]
    for h in range(hps):
        s_new = jnp.sum(q8[h] * kn_ref[0, h], axis=-1, keepdims=True)
        m = s_new
        for s in scores[h]:
            m = jnp.maximum(m, jnp.max(s, axis=-1, keepdims=True))
        e_new = jnp.exp(s_new - m)
        l = e_new
        acc = e_new * vn_ref[0, h]
        for s, vr in zip(scores[h], v_refs[h * n:(h + 1) * n]):
            e = jnp.exp(s - m)
            l = l + jnp.sum(e, axis=-1, keepdims=True)
            acc = acc + _dot(e.astype(BF16), vr[0, 0, 0].astype(BF16))
        o_ref[0, h] = (acc / l)[0:1, :]


def _moba_sample_attn(page_table, idx, q, k_new, v_new, pool_k, pool_v):
    nbat, hh, _, d = q.shape
    n_sel = idx.shape[1]
    idx = idx.reshape(nbat, n_sel * hh)
    ps = pool_k.shape[3]
    ppb = MOBA_BLOCK // PAGE_SIZE
    hps = _SAMPLE_HEADS_PER_STEP
    assert hh % hps == 0

    def page_map(b, hp, pt, ix, *, dh, j, r):
        h = hp * hps + dh
        return (0, pt[b, ix[b, j * hh + h] * ppb + r], h, 0, 0)

    page_specs = [pl.BlockSpec((1, 1, 1, ps, d), functools.partial(page_map, dh=dh, j=j, r=r))
                  for dh in range(hps) for j in range(n_sel) for r in range(ppb)]
    tok = pl.BlockSpec((1, hps, 1, d), lambda b, hp, pt, ix: (b, hp, 0, 0))
    gs = pltpu.PrefetchScalarGridSpec(
        num_scalar_prefetch=2, grid=(nbat, hh // hps),
        in_specs=[tok, tok, tok] + page_specs + page_specs,
        out_specs=tok)
    n_pg = len(page_specs)
    return pl.pallas_call(
        _moba_sample_attn_kernel, grid_spec=gs,
        out_shape=jax.ShapeDtypeStruct((nbat, hh, 1, d), F32),
        compiler_params=_params("parallel", "arbitrary"),
        name="moba_sample_attn",
    )(page_table, idx, q, k_new, v_new, *([pool_k] * n_pg), *([pool_v] * n_pg))


def _cross_sample_kernel(q_ref, mk_ref, mv_ref, o_ref):
    q = q_ref[0]
    outs = []
    for hd in range(CROSS_HEADS):
        sl = slice(hd * HEAD_DIM, (hd + 1) * HEAD_DIM)
        q16 = jnp.broadcast_to(q[:, sl], (8, HEAD_DIM)).astype(BF16)
        s = _dot_nt(q16, mk_ref[0, :, hd, :].astype(BF16)) * HEAD_DIM ** -0.5
        e = jnp.exp(s - jnp.max(s, axis=-1, keepdims=True))
        p = e / jnp.sum(e, axis=-1, keepdims=True)
        outs.append(_dot(p.astype(BF16), mv_ref[0, :, hd, :].astype(BF16))[0:1, :])
    o_ref[0] = jnp.concatenate(outs, axis=-1)


def _cross_sample(q, mem_k, mem_v):
    nbat = q.shape[0]
    row = lambda a: pl.BlockSpec((1,) + a.shape[1:], lambda b: (b,) + (0,) * (a.ndim - 1))
    return pl.pallas_call(
        _cross_sample_kernel,
        grid=(nbat,),
        in_specs=[row(q), row(mem_k), row(mem_v)],
        out_specs=row(q),
        out_shape=jax.ShapeDtypeStruct(q.shape, F32),
        compiler_params=_params("parallel"),
        name="cross_sample",
    )(q, mem_k, mem_v)


def _rope_tables(pos):
    half = ROPE_DIM // 2
    inv_freq = ROPE_THETA ** (-jnp.arange(0, ROPE_DIM, 2, dtype=F32) / ROPE_DIM)
    ang = pos.astype(F32)[:, None] * inv_freq[None, :]
    cos, sin = jnp.cos(ang), jnp.sin(ang)
    n = pos.shape[0]
    cos_t = jnp.concatenate([cos, cos, jnp.ones((n, HEAD_DIM - ROPE_DIM), F32)], axis=-1)
    sin_t = jnp.concatenate([-sin, sin, jnp.zeros((n, HEAD_DIM - ROPE_DIM), F32)], axis=-1)
    return cos_t, sin_t


def kernel(x_prompt, x_sample, cache_moba_k, cache_moba_v, page_table, state_gdn, state_conv, cache_mem_k, cache_mem_v, mem_prompt, w_ffn1_norm, w_ffn1_gu, w_ffn1_down, w_mix_norm, w_in, gdn_conv_w, gdn_a_log, gdn_dt_bias, gdn_out_norm, w_out, w_cross_norm, w_cross_q, w_cross_out, w_mem_norm, w_mem_kv, w_ffn2_norm, w_ffn2_gu, w_ffn2_down, w_final_norm):
    assert x_prompt.shape[0] == 1 and x_sample.shape[1] == 1 and w_in.shape[0] == 1
    s_len, d_model = x_prompt.shape[1], x_prompt.shape[2]
    nbat = x_sample.shape[0]
    past_len = page_table.shape[1] * PAGE_SIZE
    assert past_len % MOBA_BLOCK == 0

    row = lambda v: v.reshape(1, -1).astype(F32)
    g_ffn1, g_mix, g_cross, g_mem, g_ffn2 = (row(w[0]) for w in (w_ffn1_norm, w_mix_norm, w_cross_norm, w_mem_norm, w_ffn2_norm))
    g_final = row(w_final_norm)
    gu1, dn1 = w_ffn1_gu[0].astype(BF16), w_ffn1_down[0].astype(BF16)
    gu2, dn2 = w_ffn2_gu[0].astype(BF16), w_ffn2_down[0].astype(BF16)
    gdn_end = 4 * GDN_WIDTH
    ba_end = gdn_end + 2 * GDN_HEADS
    w_main = [w_in[0][:, :gdn_end].astype(BF16), w_in[0][:, ba_end:].astype(BF16)]
    w_ba = [jnp.pad(w_in[0][:, gdn_end:ba_end], ((0, 0), (0, LANES - 2 * GDN_HEADS))).astype(BF16)]
    wo_gdn, wo_moba = w_out[0][:GDN_WIDTH].astype(BF16), w_out[0][GDN_WIDTH:].astype(BF16)
    wq_c, wo_c = w_cross_q[0].astype(BF16), w_cross_out[0].astype(BF16)
    w_kv = w_mem_kv[0].astype(BF16)
    conv_w = gdn_conv_w[0]
    lane_pad = lambda v: jnp.pad(v.reshape(1, -1).astype(F32), ((0, 0), (GDN_HEADS, LANES - 2 * GDN_HEADS)))
    alog_row, dtb_row = lane_pad(gdn_a_log[0]), lane_pad(gdn_dt_bias[0])
    onorm = row(gdn_out_norm[0])
    moba_col0 = gdn_end

    xp = x_prompt[0]
    tm = _tile(s_len, 512)
    half = nbat // 2
    h, kmean_lo = _ffn(xp, g_ffn1, gu1, dn1, g_final, final_norm=False, tm=tm, tf=512,
                       paged=(page_table, cache_moba_k, 0, half))
    cols = _rms_matmul(h, g_mix, w_main, tm=_tile(s_len, 1024), tn=1024)
    ba = _rms_matmul(h, g_mix, w_ba, tm=_tile(s_len, 1024), tn=LANES)
    o_gdn, p_gdn_state, conv_tail = _gdn_prompt(cols, ba, conv_w, alog_row, dtb_row, onorm, chunks_per_step=2)
    q16, p_k, k16, p_v, vt16, bias = _moba_prep(cols, moba_col0)
    o_moba = _moba_attn(q16, k16, vt16, bias, hg=4)
    mem_kv = _rms_matmul(mem_prompt[0], g_mem, [w_kv], tm=mem_prompt.shape[1], tn=2 * CROSS_WIDTH)
    mem_k, mem_v = mem_kv[:, :CROSS_WIDTH], mem_kv[:, CROSS_WIDTH:]
    h = _cross_prompt(h, o_gdn, o_moba, wo_gdn, wo_moba, g_cross, wq_c, mem_k.astype(BF16), mem_v.astype(BF16), wo_c, tm=tm)
    y_prompt, kmean_hi = _ffn(h, g_ffn2, gu2, dn2, g_final, final_norm=True, tm=tm, tf=512,
                              paged=(page_table, cache_moba_k, half, nbat - half))
    kmean = jnp.concatenate([kmean_lo, kmean_hi], axis=0)

    xs = x_sample[:, 0]
    hs = _ffn(xs, g_ffn1, gu1, dn1, g_final, final_norm=False, tm=nbat, tf=512)
    cols_s = _rms_matmul(hs, g_mix, w_main, tm=nbat, tn=1024)
    ba_s = _rms_matmul(hs, g_mix, w_ba, tm=nbat, tn=LANES)
    o_gdn_s, s_gdn_state, s_conv = _gdn_sample(
        cols_s[:, None, :GDN_CONV_CH], state_conv[0], cols_s[:, None, GDN_CONV_CH:gdn_end], ba_s[:, None, :],
        conv_w, alog_row, dtb_row, onorm, state_gdn[0])
    mq, mk, mv = (cols_s[:, moba_col0 + j * MOBA_WIDTH:moba_col0 + (j + 1) * MOBA_WIDTH].reshape(nbat, MOBA_HEADS, HEAD_DIM) for j in range(3))
    cos_s, sin_s = _rope_tables(jnp.full((1,), past_len))
    n_sel = min(MOBA_TOPK, past_len // MOBA_BLOCK)
    q_r, k_r, idx = _moba_sample_gate(mq, mk, cos_s, sin_s, kmean, n_sel)
    o_moba_s = _moba_sample_attn(page_table, idx, q_r[:, :, None, :], k_r[:, :, None, :], mv[:, :, None, :],
                                 cache_moba_k, cache_moba_v)
    hs = _proj_res(hs, [(o_gdn_s[:, 0], wo_gdn), (o_moba_s.reshape(nbat, MOBA_WIDTH), wo_moba)], tm=nbat)
    q_c = _rms_matmul(hs, g_cross, [wq_c], tm=nbat, tn=CROSS_WIDTH)
    o_c = _cross_sample(q_c[:, None, :], cache_mem_k[0], cache_mem_v[0])
    hs = _proj_res(hs, [(o_c[:, 0], wo_c)], tm=nbat)
    y_sample = _ffn(hs, g_ffn2, gu2, dn2, g_final, final_norm=True, tm=nbat, tf=512)

    return (y_prompt[None], y_sample[:, None, :],
            p_k[None, None], p_v[None, None],
            p_gdn_state[None, None], conv_tail[8 - (CONV_WIDTH - 1):][None, None],
            mem_k.reshape(1, 1, -1, CROSS_HEADS, HEAD_DIM), mem_v.reshape(1, 1, -1, CROSS_HEADS, HEAD_DIM),
            k_r[None, :, :, None, :], mv[None, :, :, None, :],
            s_gdn_state[None], s_conv[None])
```

```python
import functools

import jax
import jax.numpy as jnp
from jax import lax
from jax.experimental import pallas as pl
from jax.experimental.pallas import tpu as pltpu

F32 = jnp.float32
BF16 = jnp.bfloat16
HIGHEST = lax.Precision.HIGHEST

HEAD_DIM = 128
GDN_HEADS = 8
MOBA_HEADS = 8
GDN_WIDTH = GDN_HEADS * HEAD_DIM
MOBA_WIDTH = MOBA_HEADS * HEAD_DIM
CONV_WIDTH = 4
GDN_CONV_CH = 3 * GDN_WIDTH
GDN_CHUNK = 64
MOBA_BLOCK = 256
MOBA_TOPK = 3
PAGE_SIZE = 128
ROPE_DIM = HEAD_DIM // 4
ROPE_THETA = 500000.0
CROSS_HEADS = 4
CROSS_WIDTH = CROSS_HEADS * HEAD_DIM
NORM_EPS = 1e-6

LANES = 128
MASK_VALUE = -1e30
LOG2_E = 1.4426950408889634
VT_ROWS = HEAD_DIM + 16
VMEM_LIMIT_BYTES = 56 * 1024 * 1024

_NT = (((1,), (1,)), ((), ()))
_TN = (((0,), (0,)), ((), ()))


def _params(*sem):
    return pltpu.CompilerParams(dimension_semantics=sem, vmem_limit_bytes=VMEM_LIMIT_BYTES)


def _rms(x, g):
    return x * lax.rsqrt(jnp.mean(x * x, axis=-1, keepdims=True) + NORM_EPS) * g


def _silu(x):
    return x * jax.nn.sigmoid(x)


def _dot(a, b):
    return jnp.dot(a, b, preferred_element_type=F32)


def _dot_nt(a, b):
    return lax.dot_general(a, b, _NT, preferred_element_type=F32)


def _dot_f32(a, b):
    return jnp.dot(a, b, precision=HIGHEST, preferred_element_type=F32)


def _col(x, c, width):
    return jnp.broadcast_to(x[:, c:c + 1], (x.shape[0], width))


def _tile(m, pref):
    return pref if m % pref == 0 else m


def _rms_matmul_kernel(x_ref, g_ref, w_ref, *refs):
    o_ref, xn_ref = refs[-3] if len(refs) == 4 else refs[0], refs[-1]

    @pl.when(pl.program_id(1) == 0)
    def _():
        xn_ref[...] = _rms(x_ref[...], g_ref[...]).astype(BF16)
        if len(refs) == 4:
            refs[2][...] = _dot(xn_ref[...], refs[0][...])

    o_ref[...] = _dot(xn_ref[...], w_ref[...])


def _rms_matmul(x, g, w, *, tm, tn, w_side=None):
    m, d = x.shape
    n = w.shape[1]
    in_specs = [pl.BlockSpec((tm, d), lambda i, j: (i, 0)),
                pl.BlockSpec((1, d), lambda i, j: (0, 0)),
                pl.BlockSpec((d, tn), lambda i, j: (0, j))]
    out_specs = [pl.BlockSpec((tm, tn), lambda i, j: (i, j))]
    out_shape = [jax.ShapeDtypeStruct((m, n), F32)]
    args = [x, g, w]
    if w_side is not None:
        in_specs.append(pl.BlockSpec(w_side.shape, lambda i, j: (0, 0)))
        out_specs.append(pl.BlockSpec((tm, w_side.shape[1]), lambda i, j: (i, 0)))
        out_shape.append(jax.ShapeDtypeStruct((m, w_side.shape[1]), F32))
        args.append(w_side)
    outs = pl.pallas_call(
        _rms_matmul_kernel,
        grid=(m // tm, n // tn),
        in_specs=in_specs, out_specs=out_specs, out_shape=out_shape,
        scratch_shapes=[pltpu.VMEM((tm, d), BF16)],
        compiler_params=_params("parallel", "arbitrary"),
        name="rms_matmul",
    )(*args)
    return outs if w_side is not None else outs[0]


_PAGES_PER_STEP = 8


def _block_key_means(page_refs, o_ref, blocks=None):
    ppb = MOBA_BLOCK // PAGE_SIZE
    for blk in (range(len(page_refs) // ppb) if blocks is None else blocks):
        tot = jnp.sum(page_refs[blk * ppb][0, 0], axis=1)
        for r in range(1, ppb):
            tot = tot + jnp.sum(page_refs[blk * ppb + r][0, 0], axis=1)
        o_ref[0, blk] = tot * (1.0 / MOBA_BLOCK)


def _ffn_kernel(*refs, final_norm, n_pages):
    if n_pages:
        refs = refs[1:]
    x_ref, g_ref, wg_ref, wu_ref, wd_ref, fg_ref = refs[:6]
    page_refs = refs[6:6 + n_pages]
    o_ref, xn_ref = refs[6 + n_pages], refs[-1]
    j = pl.program_id(1)

    @pl.when(j == 0)
    def _():
        xn_ref[...] = _rms(x_ref[...], g_ref[...]).astype(BF16)
        o_ref[...] = jnp.zeros_like(o_ref)

    xn = xn_ref[...]
    tf = wg_ref.shape[1]
    parts = 2 if n_pages else 1
    acc = None
    for s in range(parts):
        cs = slice(s * tf // parts, (s + 1) * tf // parts)
        act = _silu(_dot(xn, wg_ref[:, cs])) * _dot(xn, wu_ref[:, cs])
        part = _dot(act.astype(BF16), wd_ref[cs, :])
        acc = part if acc is None else acc + part
        if n_pages:
            nblk = n_pages // (MOBA_BLOCK // PAGE_SIZE)
            _block_key_means(page_refs, refs[7 + n_pages], range(s * nblk // parts, (s + 1) * nblk // parts))
    o_ref[...] += acc

    @pl.when(j == pl.num_programs(1) - 1)
    def _():
        h = x_ref[...] + 0.5 * o_ref[...]
        if final_norm:
            h = _rms(h, fg_ref[...])
        o_ref[...] = h


def _kmean_slots(page_table_rows, n_pages):
    return page_table_rows * (n_pages // _PAGES_PER_STEP)


def _ffn(x, g, w_gu, w_down, fg, *, final_norm, tm, tf, paged=None):
    m, d = x.shape
    f = w_down.shape[0]
    nf = f // tf
    grid = (m // tm, nf)
    in_specs = [pl.BlockSpec((tm, d), lambda i, j, *_: (i, 0)),
                pl.BlockSpec((1, d), lambda i, j, *_: (0, 0)),
                pl.BlockSpec((d, tf), lambda i, j, *_: (0, j)),
                pl.BlockSpec((d, tf), lambda i, j, *_: (0, nf + j)),
                pl.BlockSpec((tf, d), lambda i, j, *_: (j, 0)),
                pl.BlockSpec((1, d), lambda i, j, *_: (0, 0))]
    out_spec = pl.BlockSpec((tm, d), lambda i, j, *_: (i, 0))
    out_shape = jax.ShapeDtypeStruct((m, d), F32)
    scratch = [pltpu.VMEM((tm, d), BF16)]
    if paged is None:
        return pl.pallas_call(
            functools.partial(_ffn_kernel, final_norm=final_norm, n_pages=0),
            grid=grid, in_specs=in_specs, out_specs=out_spec, out_shape=out_shape, scratch_shapes=scratch,
            compiler_params=_params("parallel", "arbitrary"), name="ffn",
        )(x, g, w_gu, w_gu, w_down, fg)

    page_table, pool_k, row0, n_rows = paged
    n_pages = page_table.shape[1]
    _, _, hh, ps, hd = pool_k.shape
    groups = n_pages // _PAGES_PER_STEP
    n_slots = _kmean_slots(n_rows, n_pages)
    assert n_pages % _PAGES_PER_STEP == 0 and n_slots <= grid[0] * grid[1]
    ppb = MOBA_BLOCK // PAGE_SIZE

    def slot(i, j):
        t = jnp.minimum(i * nf + j, n_slots - 1)
        return lax.div(t, groups), lax.rem(t, groups)

    def page_map(i, j, pt, *, r):
        b, c = slot(i, j)
        return (0, pt[row0 + b, c * _PAGES_PER_STEP + r], 0, 0, 0)

    def kmean_map(i, j, pt):
        b, c = slot(i, j)
        return (b, c, 0, 0)

    page_specs = [pl.BlockSpec((1, 1, hh, ps, hd), functools.partial(page_map, r=r)) for r in range(_PAGES_PER_STEP)]
    gs = pltpu.PrefetchScalarGridSpec(
        num_scalar_prefetch=1, grid=grid, in_specs=in_specs + page_specs,
        out_specs=[out_spec, pl.BlockSpec((1, _PAGES_PER_STEP // ppb, hh, hd), kmean_map)],
        scratch_shapes=scratch)
    return pl.pallas_call(
        functools.partial(_ffn_kernel, final_norm=final_norm, n_pages=_PAGES_PER_STEP),
        grid_spec=gs,
        out_shape=[out_shape, jax.ShapeDtypeStruct((n_rows, n_pages // ppb, hh, hd), F32)],
        compiler_params=_params("arbitrary", "arbitrary"), name="ffn_kmean",
    )(page_table, x, g, w_gu, w_gu, w_down, fg, *([pool_k] * _PAGES_PER_STEP))


def _proj_res_kernel(*refs):
    res_ref, o_ref = refs[0], refs[-1]
    pairs = refs[1:-1]
    acc = res_ref[...]
    for a_ref, w_ref in zip(pairs[0::2], pairs[1::2]):
        acc = acc + _dot(a_ref[...].astype(BF16), w_ref[...])
    o_ref[...] = acc


def _proj_res(res, pairs, *, tm):
    m, d = res.shape
    in_specs = [pl.BlockSpec((tm, d), lambda i: (i, 0))]
    args = [res]
    for a, w in pairs:
        in_specs.append(pl.BlockSpec((tm, a.shape[1]), lambda i: (i, 0)))
        in_specs.append(pl.BlockSpec(w.shape, lambda i: (0, 0)))
        args += [a, w]
    return pl.pallas_call(
        _proj_res_kernel,
        grid=(m // tm,),
        in_specs=in_specs,
        out_specs=pl.BlockSpec((tm, d), lambda i: (i, 0)),
        out_shape=jax.ShapeDtypeStruct((m, d), F32),
        compiler_params=_params("parallel"),
        name="proj_res",
    )(*args)


def _cross_kernel(res_ref, a_ref, b_ref, wa_ref, wb_ref, g_ref, wq_ref, mk_ref, mv_ref, wo_ref, o_ref):
    h = res_ref[...] + _dot(a_ref[...].astype(BF16), wa_ref[...]) + _dot(b_ref[...].astype(BF16), wb_ref[...])
    q = _dot(_rms(h, g_ref[...]).astype(BF16), wq_ref[...])
    outs = []
    for hd in range(CROSS_HEADS):
        sl = slice(hd * HEAD_DIM, (hd + 1) * HEAD_DIM)
        s = _dot_nt(q[:, sl].astype(BF16), mk_ref[:, sl]) * HEAD_DIM ** -0.5
        e = jnp.exp(s - jnp.max(s, axis=-1, keepdims=True))
        p = e / jnp.sum(e, axis=-1, keepdims=True)
        outs.append(_dot(p.astype(BF16), mv_ref[:, sl]).astype(BF16))
    o_ref[...] = h + _dot(jnp.concatenate(outs, axis=-1), wo_ref[...])


def _cross_prompt(res, a, b, wa, wb, g, wq, mem_k, mem_v, wo, *, tm):
    m, d = res.shape
    rows = lambda x: pl.BlockSpec((tm, x.shape[1]), lambda i: (i, 0))
    full = lambda x: pl.BlockSpec(x.shape, lambda i: (0, 0), pipeline_mode=pl.Buffered(1))
    return pl.pallas_call(
        _cross_kernel,
        grid=(m // tm,),
        in_specs=[rows(res), rows(a), rows(b), full(wa), full(wb), full(g), full(wq), full(mem_k), full(mem_v), full(wo)],
        out_specs=pl.BlockSpec((tm, d), lambda i: (i, 0)),
        out_shape=jax.ShapeDtypeStruct((m, d), F32),
        compiler_params=_params("parallel"),
        name="cross_prompt",
    )(res, a, b, wa, wb, g, wq, mem_k, mem_v, wo)


def _split(x):
    hi = x.astype(BF16)
    return hi, (x - hi.astype(F32)).astype(BF16)


def _dot_split(a, b):
    return _dot(a[0], b[0]) + _dot(a[0], b[1]) + _dot(a[1], b[0])


def _gdn_prompt_kernel(qkv_ref, z_ref, ba_ref, cw_ref, alog_ref, dtb_ref, onorm_ref,
                       o_ref, sfin_ref, convn_ref, tail_ref, y_ref, s_ref):
    c = GDN_CHUNK
    rows = qkv_ref.shape[0]
    nck = rows // c
    i = pl.program_id(0)

    @pl.when(i == 0)
    def _():
        tail_ref[...] = jnp.zeros_like(tail_ref)
        s_ref[...] = jnp.zeros_like(s_ref)

    x = qkv_ref[...]
    tail = tail_ref[...]
    row8 = lax.broadcasted_iota(jnp.int32, tail.shape, 0)
    y = cw_ref[CONV_WIDTH - 1:CONV_WIDTH, :] * x
    for k in range(1, CONV_WIDTH):
        rolled = pltpu.roll(x, k, 0)
        head = jnp.where(row8 < k, pltpu.roll(tail, k, 0), rolled[0:8])
        y = y + cw_ref[CONV_WIDTH - 1 - k:CONV_WIDTH - k, :] * jnp.concatenate([head, rolled[8:]], axis=0)
    y_ref[...] = _silu(y)
    tail_ref[...] = x[rows - 8:]
    convn_ref[...] = x[rows - 8:]

    ba = ba_ref[...]
    beta = jax.nn.sigmoid(ba)
    g = -jnp.exp(alog_ref[...]) * jax.nn.softplus(ba + dtb_ref[...])
    row = lax.broadcasted_iota(jnp.int32, (c, c), 0)
    colv = lax.broadcasted_iota(jnp.int32, (c, c), 1)
    causal = row >= colv
    strict = row > colv
    eye = (row == colv).astype(F32)
    tri = causal.astype(F32)
    heads = range(GDN_HEADS)

    work = []
    for ck in range(nck):
        r0 = ck * c
        gc = _dot_f32(tri, g[r0:r0 + c])
        gc_t = gc.T
        eg = jnp.exp(gc)
        g_last = gc[c - 1:c, :]
        eg_last = jnp.exp(g_last)
        k_decay = jnp.exp(g_last - gc)
        for h in heads:
            gh = GDN_HEADS + h
            qh = y_ref[r0:r0 + c, h * HEAD_DIM:(h + 1) * HEAD_DIM]
            kh = y_ref[r0:r0 + c, GDN_WIDTH + h * HEAD_DIM:GDN_WIDTH + (h + 1) * HEAD_DIM]
            vh = y_ref[r0:r0 + c, 2 * GDN_WIDTH + h * HEAD_DIM:2 * GDN_WIDTH + (h + 1) * HEAD_DIM]
            qn = qh * lax.rsqrt(jnp.sum(qh * qh, axis=-1, keepdims=True) + NORM_EPS) * HEAD_DIM ** -0.5
            kn = kh * lax.rsqrt(jnp.sum(kh * kh, axis=-1, keepdims=True) + NORM_EPS)
            b_col = _col(beta[r0:r0 + c], h, HEAD_DIM)
            eg_col = _col(eg, gh, HEAD_DIM)
            diff = _col(gc, gh, c) - jnp.broadcast_to(gc_t[gh:gh + 1, :], (c, c))
            decay = jnp.exp(jnp.where(causal, diff, -jnp.inf))
            kb = kn * b_col
            kn16 = kn.astype(BF16)
            qn16 = qn.astype(BF16)
            work.append(dict(
                a=_dot_nt(kb.astype(BF16), kn16) * jnp.where(strict, decay, 0.0),
                attn=(_dot_nt(qn16, kn16) * decay).astype(BF16),
                vb=(vh * b_col).astype(BF16),
                kbg=(kb * eg_col).astype(BF16),
                qg=(qn * eg_col).astype(BF16),
                kd=(kn * _col(k_decay, gh, HEAD_DIM)).astype(BF16),
                eg_last=eg_last[:, gh:gh + 1]))

    ps = [_split(-wk["a"]) for wk in work]
    ts = [eye - wk["a"] for wk in work]
    for _ in range(5):
        ps = [_split(_dot_split(p, p)) for p in ps]
        ts = [t + _dot_split(_split(t), p) for t, p in zip(ts, ps)]
    for wk, t in zip(work, ts):
        t16 = t.astype(BF16)
        wk["u"] = _dot(t16, wk["vb"])
        wk["w"] = _dot(t16, wk["kbg"]).astype(BF16)

    for ck in range(nck):
        r0 = ck * c
        wks = work[ck * GDN_HEADS:(ck + 1) * GDN_HEADS]
        states = [s_ref[h] for h in heads]
        s16 = [s.astype(BF16) for s in states]
        v_new = [(wk["u"] - _dot(wk["w"], s16[h])).astype(BF16) for h, wk in zip(heads, wks)]
        outs = [_dot(wk["qg"], s16[h]) + _dot(wk["attn"], v_new[h]) for h, wk in zip(heads, wks)]
        for h, wk in zip(heads, wks):
            s_ref[h] = states[h] * wk["eg_last"] + lax.dot_general(wk["kd"], v_new[h], _TN, preferred_element_type=F32)
        for h, o in zip(heads, outs):
            sl = slice(h * HEAD_DIM, (h + 1) * HEAD_DIM)
            o = o * lax.rsqrt(jnp.mean(o * o, axis=-1, keepdims=True) + NORM_EPS) * onorm_ref[...]
            o_ref[r0:r0 + c, sl] = o * _silu(z_ref[r0:r0 + c, sl])

    @pl.when(i == pl.num_programs(0) - 1)
    def _():
        sfin_ref[...] = s_ref[...]


def _gdn_prompt(cols, ba, conv_w, alog_row, dtb_row, onorm, *, chunks_per_step):
    s_len = cols.shape[0]
    c = GDN_CHUNK * chunks_per_step
    assert s_len % c == 0
    full = lambda a: pl.BlockSpec(a.shape, lambda i: (0,) * a.ndim)
    return pl.pallas_call(
        _gdn_prompt_kernel,
        grid=(s_len // c,),
        in_specs=[pl.BlockSpec((c, GDN_CONV_CH), lambda i: (i, 0)),
                  pl.BlockSpec((c, GDN_WIDTH), lambda i: (i, GDN_CONV_CH // GDN_WIDTH)),
                  pl.BlockSpec((c, LANES), lambda i: (i, 0)),
                  full(conv_w), full(alog_row), full(dtb_row), full(onorm)],
        out_specs=[pl.BlockSpec((c, GDN_WIDTH), lambda i: (i, 0)),
                   pl.BlockSpec((GDN_HEADS, HEAD_DIM, HEAD_DIM), lambda i: (0, 0, 0)),
                   pl.BlockSpec((8, GDN_CONV_CH), lambda i: (0, 0))],
        out_shape=[jax.ShapeDtypeStruct((s_len, GDN_WIDTH), F32),
                   jax.ShapeDtypeStruct((GDN_HEADS, HEAD_DIM, HEAD_DIM), F32),
                   jax.ShapeDtypeStruct((8, GDN_CONV_CH), F32)],
        scratch_shapes=[pltpu.VMEM((8, GDN_CONV_CH), F32),
                        pltpu.VMEM((c, GDN_CONV_CH), F32),
                        pltpu.VMEM((GDN_HEADS, HEAD_DIM, HEAD_DIM), F32)],
        compiler_params=_params("arbitrary"),
        name="gdn_prompt",
    )(cols, cols, ba, conv_w, alog_row, dtb_row, onorm)


def _gdn_sample_kernel(x_ref, cs_ref, z_ref, ba_ref, cw_ref, alog_ref, dtb_ref, onorm_ref, s0_ref,
                       o_ref, s_ref, convn_ref):
    x = x_ref[0]
    cs = cs_ref[0]
    y = cw_ref[CONV_WIDTH - 1:CONV_WIDTH, :] * x
    for j in range(CONV_WIDTH - 1):
        y = y + cw_ref[j:j + 1, :] * cs[j:j + 1, :]
    y = _silu(y)
    convn_ref[0] = jnp.concatenate([cs[1:CONV_WIDTH - 1, :], x], axis=0)

    ba = ba_ref[0]
    beta = jax.nn.sigmoid(ba)
    eg = jnp.exp(-jnp.exp(alog_ref[...]) * jax.nn.softplus(ba + dtb_ref[...]))
    z = z_ref[0]

    def head_rows(base):
        return jnp.concatenate([y[:, base + h * HEAD_DIM:base + (h + 1) * HEAD_DIM] for h in range(GDN_HEADS)], axis=0)

    q8 = head_rows(0)
    k8 = head_rows(GDN_WIDTH)
    v8 = head_rows(2 * GDN_WIDTH)
    q8 = q8 * lax.rsqrt(jnp.sum(q8 * q8, axis=-1, keepdims=True) + NORM_EPS) * HEAD_DIM ** -0.5
    k8 = k8 * lax.rsqrt(jnp.sum(k8 * k8, axis=-1, keepdims=True) + NORM_EPS)
    qk8 = jnp.sum(q8 * k8, axis=-1, keepdims=True)
    q_t = q8.T
    k_t = k8.T
    outs = []
    for h in range(GDN_HEADS):
        s = s0_ref[0, h]
        k_col = _col(k_t, h, HEAD_DIM)
        q_col = _col(q_t, h, HEAD_DIM)
        e = eg[:, GDN_HEADS + h:GDN_HEADS + h + 1]
        k_s = jnp.sum(k_col * s, axis=0, keepdims=True)
        q_s = jnp.sum(q_col * s, axis=0, keepdims=True)
        v_new = beta[:, h:h + 1] * (v8[h:h + 1, :] - e * k_s)
        o = e * q_s + qk8[h:h + 1, :] * v_new
        s_ref[0, h] = s * e + k_col * v_new
        o = o * lax.rsqrt(jnp.mean(o * o, axis=-1, keepdims=True) + NORM_EPS) * onorm_ref[...]
        outs.append(o * _silu(z[:, h * HEAD_DIM:(h + 1) * HEAD_DIM]))
    o_ref[0] = jnp.concatenate(outs, axis=-1)


def _gdn_sample(x, conv_state, z, ba, conv_w, alog_row, dtb_row, onorm, s0):
    nb = x.shape[0]
    full = lambda a: pl.BlockSpec(a.shape, lambda b: (0,) * a.ndim)
    row = lambda a: pl.BlockSpec((1,) + a.shape[1:], lambda b: (b,) + (0,) * (a.ndim - 1))
    return pl.pallas_call(
        _gdn_sample_kernel,
        grid=(nb,),
        in_specs=[row(x), row(conv_state), row(z), row(ba), full(conv_w), full(alog_row), full(dtb_row),
                  full(onorm), row(s0)],
        out_specs=[pl.BlockSpec((1, 1, GDN_WIDTH), lambda b: (b, 0, 0)), row(s0), row(conv_state)],
        out_shape=[jax.ShapeDtypeStruct((nb, 1, GDN_WIDTH), F32),
                   jax.ShapeDtypeStruct(s0.shape, F32),
                   jax.ShapeDtypeStruct(conv_state.shape, F32)],
        compiler_params=_params("parallel"),
        name="gdn_sample",
    )(x, conv_state, z, ba, conv_w, alog_row, dtb_row, onorm, s0)


def _rope(x, cos, sin_signed):
    lane = lax.broadcasted_iota(jnp.int32, x.shape, x.ndim - 1)
    half = ROPE_DIM // 2
    partner = jnp.where(lane < half, pltpu.roll(x, LANES - half, x.ndim - 1), pltpu.roll(x, half, x.ndim - 1))
    return x * cos + partner * sin_signed


def _moba_prep_kernel(q_ref, k_ref, v_ref, cos_in_ref, sin_in_ref, cos_blk_ref, sin_blk_ref,
                      q16_ref, kf_ref, k16_ref, vf_ref, vt16_ref, bias_ref, kmean_ref, *, n_sel):
    i = pl.program_id(0)
    nb = kmean_ref.shape[1]
    t = q_ref.shape[0]

    @pl.when(i == 0)
    def _():
        kmean_ref[...] = jnp.zeros_like(kmean_ref)

    cb, sb = cos_blk_ref[pl.ds(i, 1), :], sin_blk_ref[pl.ds(i, 1), :]
    cos = cb * cos_in_ref[...] - sb * sin_in_ref[...]
    sin = sb * cos_in_ref[...] + cb * sin_in_ref[...]
    blk = lax.broadcasted_iota(jnp.int32, (nb, t), 0)
    past = blk < i
    for h in range(MOBA_HEADS):
        sl = slice(h * HEAD_DIM, (h + 1) * HEAD_DIM)
        q = _rope(q_ref[:, sl], cos, sin)
        k = _rope(k_ref[:, sl], cos, sin)
        v = v_ref[:, sl]
        q16_ref[h] = (q * (HEAD_DIM ** -0.5 * LOG2_E)).astype(BF16)
        kf_ref[h] = k
        k16_ref[h] = k.astype(BF16)
        vf_ref[h] = v
        vt16_ref[h, 0, 0:HEAD_DIM, :] = v.T.astype(BF16)
        vt16_ref[h, 0, HEAD_DIM:VT_ROWS, :] = (lax.broadcasted_iota(jnp.int32, (VT_ROWS - HEAD_DIM, t), 0) == 0).astype(BF16)
        gate = lax.dot_general(kmean_ref[h], q, _NT, precision=HIGHEST, preferred_element_type=F32)
        gate = jnp.where(past, gate, -jnp.inf)
        chosen = jnp.zeros((nb, t), F32)
        for _ in range(n_sel):
            top = jnp.max(gate, axis=0, keepdims=True)
            first = jnp.min(jnp.where(gate == top, blk, nb), axis=0, keepdims=True)
            pick = (blk == first) & (top > -jnp.inf)
            chosen = jnp.where(pick, 1.0, chosen)
            gate = jnp.where(pick, -jnp.inf, gate)
        bias_ref[h] = jnp.where(chosen > 0.0, 0.0, MASK_VALUE)
        kmean_ref[h, pl.ds(i, 1), :] = jnp.mean(k, axis=0, keepdims=True)


def _moba_prep(cols, col0):
    s_len = cols.shape[0]
    t = MOBA_BLOCK
    assert s_len % t == 0
    nb = s_len // t
    hh = MOBA_HEADS
    cb = col0 // MOBA_WIDTH
    n_sel = min(MOBA_TOPK, nb - 1)
    cos_in, sin_in = _rope_tables(jnp.arange(t))
    cos_blk, sin_blk = _rope_tables(jnp.arange(nb) * t)
    full = lambda a: pl.BlockSpec(a.shape, lambda i: (0, 0))
    return pl.pallas_call(
        functools.partial(_moba_prep_kernel, n_sel=n_sel),
        grid=(nb,),
        in_specs=[pl.BlockSpec((t, MOBA_WIDTH), lambda i: (i, cb)),
                  pl.BlockSpec((t, MOBA_WIDTH), lambda i: (i, cb + 1)),
                  pl.BlockSpec((t, MOBA_WIDTH), lambda i: (i, cb + 2)),
                  full(cos_in), full(sin_in), full(cos_blk), full(sin_blk)],
        out_specs=[pl.BlockSpec((hh, t, HEAD_DIM), lambda i: (0, i, 0)),
                   pl.BlockSpec((hh, t, HEAD_DIM), lambda i: (0, i, 0)),
                   pl.BlockSpec((hh, t, HEAD_DIM), lambda i: (0, i, 0)),
                   pl.BlockSpec((hh, t, HEAD_DIM), lambda i: (0, i, 0)),
                   pl.BlockSpec((hh, 1, VT_ROWS, t), lambda i: (0, i, 0, 0)),
                   pl.BlockSpec((hh, nb, t), lambda i: (0, 0, i))],
        out_shape=[jax.ShapeDtypeStruct((hh, s_len, HEAD_DIM), BF16),
                   jax.ShapeDtypeStruct((hh, s_len, HEAD_DIM), F32),
                   jax.ShapeDtypeStruct((hh, s_len, HEAD_DIM), BF16),
                   jax.ShapeDtypeStruct((hh, s_len, HEAD_DIM), F32),
                   jax.ShapeDtypeStruct((hh, nb, VT_ROWS, t), BF16),
                   jax.ShapeDtypeStruct((hh, nb, s_len), F32)],
        scratch_shapes=[pltpu.VMEM((hh, nb, HEAD_DIM), F32)],
        compiler_params=_params("arbitrary"),
        name="moba_prep",
    )(cols, cols, cols, cos_in, sin_in, cos_blk, sin_blk)


_KV_PAIR = 2


def _moba_attn_kernel(q_ref, k_ref, vt_ref, bias_ref, o_ref, acc_ref, s_ref, p_ref):
    hg, tq, _ = q_ref.shape
    tk = vt_ref.shape[3]
    own = pl.program_id(1)
    diag = own // _KV_PAIR
    kpos = lax.broadcasted_iota(jnp.int32, (tk, tq), 0)
    qpos = lax.broadcasted_iota(jnp.int32, (tk, tq), 1)
    causal_bias = jnp.where(kpos <= qpos, 0.0, MASK_VALUE)

    def scores(j, slot, diagonal=False):
        g = diag if diagonal else jnp.minimum(j - 1, diag)
        real = j <= diag
        off = pl.multiple_of(g * (_KV_PAIR * tk), _KV_PAIR * tk)
        for h in range(hg):
            s = _dot_nt(k_ref[h, pl.ds(off, _KV_PAIR * tk), :], q_ref[h])
            for r in range(_KV_PAIR):
                kb = g * _KV_PAIR + r
                if diagonal:
                    row = jnp.broadcast_to(bias_ref[h, pl.ds(kb, 1), :], (tk, tq))
                    bias = jnp.where(kb == own, causal_bias, row)
                else:
                    bias = jnp.where(real, bias_ref[h, pl.ds(kb, 1), :], MASK_VALUE)
                s_ref[slot, h, r * tk:(r + 1) * tk, :] = s[r * tk:(r + 1) * tk] + bias

    def softmax(slot, ms):
        new, alphas = [], []
        for h in range(hg):
            s = s_ref[slot, h]
            m_new = jnp.maximum(ms[h], jnp.max(s, axis=0, keepdims=True))
            p_ref[slot, h] = jnp.exp2(s - m_new).astype(BF16)
            new.append(m_new)
            alphas.append(jnp.exp2(ms[h] - m_new))
        return tuple(new), tuple(alphas)

    def accumulate(j, slot, alphas):
        g = jnp.where(j == 0, diag, jnp.minimum(j - 1, diag))
        for h in range(hg):
            pv = _dot(vt_ref[h, g * _KV_PAIR], p_ref[slot, h, 0:tk, :])
            for r in range(1, _KV_PAIR):
                pv = pv + _dot(vt_ref[h, g * _KV_PAIR + r], p_ref[slot, h, r * tk:(r + 1) * tk, :])
            acc_ref[h] = acc_ref[h] * alphas[h] + pv

    acc_ref[...] = jnp.zeros_like(acc_ref)
    m0 = jnp.full((1, tq), MASK_VALUE, F32)
    scores(0, 0, diagonal=True)
    scores(1, 1)
    carry = softmax(0, (m0,) * hg)

    def step(j, cur, carry):
        ms, alphas = carry
        scores(j + 1, 1 - cur)
        new = softmax(cur, ms)
        accumulate(j - 1, 1 - cur, alphas)
        return new

    def two_steps(t, carry):
        return step(2 * t + 2, 0, step(2 * t + 1, 1, carry))

    half = (diag + 1) // 2
    _, alphas = lax.fori_loop(0, half, two_steps, carry)
    accumulate(2 * half, 0, alphas)
    for h in range(hg):
        acc = acc_ref[h]
        o_ref[:, h * HEAD_DIM:(h + 1) * HEAD_DIM] = (acc[0:HEAD_DIM] / acc[HEAD_DIM:HEAD_DIM + 1]).T


def _moba_attn(q16, k16, vt16, bias, *, hg):
    hh, s_len, d = q16.shape
    nb, rows, tk = vt16.shape[1:]
    assert nb % _KV_PAIR == 0 and hh % hg == 0
    once = pl.Buffered(1)
    return pl.pallas_call(
        _moba_attn_kernel,
        grid=(hh // hg, nb),
        in_specs=[pl.BlockSpec((hg, tk, d), lambda h, i: (h, i, 0)),
                  pl.BlockSpec((hg, s_len, d), lambda h, i: (h, 0, 0), pipeline_mode=once),
                  pl.BlockSpec((hg, nb, rows, tk), lambda h, i: (h, 0, 0, 0), pipeline_mode=once),
                  pl.BlockSpec((hg, nb, tk), lambda h, i: (h, 0, i))],
        out_specs=pl.BlockSpec((tk, hg * d), lambda h, i: (i, h)),
        out_shape=jax.ShapeDtypeStruct((s_len, hh * d), F32),
        scratch_shapes=[pltpu.VMEM((hg, rows, tk), F32),
                        pltpu.VMEM((2, hg, _KV_PAIR * tk, tk), F32),
                        pltpu.VMEM((2, hg, _KV_PAIR * tk, tk), BF16)],
        compiler_params=_params("parallel", "arbitrary"),
        name="moba_attn",
    )(q16, k16, vt16, bias)


def _moba_sample_gate_kernel(q_ref, k_ref, cos_ref, sin_ref, km_ref, qr_ref, kr_ref, idx_ref, *, n_sel):
    q = _rope(q_ref[0], cos_ref[...], sin_ref[...])
    kr_ref[0] = _rope(k_ref[0], cos_ref[...], sin_ref[...])
    qr_ref[0] = q
    gate = jnp.sum(km_ref[0] * q, axis=-1)
    nb = gate.shape[0]
    blk = lax.broadcasted_iota(jnp.int32, gate.shape, 0)
    rows = []
    for _ in range(n_sel):
        top = jnp.max(gate, axis=0, keepdims=True)
        first = jnp.min(jnp.where(gate == top, blk, nb), axis=0, keepdims=True)
        rows.append(first)
        gate = jnp.where(blk == first, -jnp.inf, gate)
    idx_ref[0] = jnp.concatenate(rows, axis=0)


def _moba_sample_gate(q, k, cos, sin, kmean, n_sel):
    nbat, hh, d = q.shape
    nb = kmean.shape[1]
    row = lambda a: pl.BlockSpec((1,) + a.shape[1:], lambda b: (b,) + (0,) * (a.ndim - 1))
    full = lambda a: pl.BlockSpec(a.shape, lambda b: (0,) * a.ndim)
    return pl.pallas_call(
        functools.partial(_moba_sample_gate_kernel, n_sel=n_sel),
        grid=(nbat,),
        in_specs=[row(q), row(k), full(cos), full(sin), row(kmean)],
        out_specs=[row(q), row(k), pl.BlockSpec((1, n_sel, hh), lambda b: (b, 0, 0))],
        out_shape=[jax.ShapeDtypeStruct(q.shape, F32), jax.ShapeDtypeStruct(k.shape, F32),
                   jax.ShapeDtypeStruct((nbat, n_sel, hh), jnp.int32)],
        compiler_params=_params("parallel"),
        name="moba_sample_gate",
    )(q, k, cos, sin, kmean)


_SAMPLE_HEADS_PER_STEP = 4


def _moba_sample_attn_kernel(pt_ref, idx_ref, q_ref, kn_ref, vn_ref, *refs):
    del pt_ref, idx_ref
    o_ref = refs[-1]
    hps = q_ref.shape[1]
    n = (len(refs) - 1) // (2 * hps)
    k_refs, v_refs = refs[:n * hps], refs[n * hps:2 * n * hps]
    q8 = [jnp.broadcast_to(q_ref[0, h] * HEAD_DIM ** -0.5, (8, HEAD_DIM)) for h in range(hps)]
    scores = [[_dot_nt(q8[h].astype(BF16), kr[0, 0, 0].astype(BF16)) for kr in k_refs[h * n:(h + 1) * n]] for h in range(hps)]
    for h in range(hps):
        s_new = jnp.sum(q8[h] * kn_ref[0, h], axis=-1, keepdims=True)
        m = s_new
        for s in scores[h]:
            m = jnp.maximum(m, jnp.max(s, axis=-1, keepdims=True))
        e_new = jnp.exp(s_new - m)
        l = e_new
        acc = e_new * vn_ref[0, h]
        for s, vr in zip(scores[h], v_refs[h * n:(h + 1) * n]):
            e = jnp.exp(s - m)
            l = l + jnp.sum(e, axis=-1, keepdims=True)
            acc = acc + _dot(e.astype(BF16), vr[0, 0, 0].astype(BF16))
        o_ref[0, h] = (acc / l)[0:1, :]


def _moba_sample_attn(page_table, idx, q, k_new, v_new, pool_k, pool_v):
    nbat, hh, _, d = q.shape
    n_sel = idx.shape[1]
    idx = idx.reshape(nbat, n_sel * hh)
    ps = pool_k.shape[3]
    ppb = MOBA_BLOCK // PAGE_SIZE
    hps = _SAMPLE_HEADS_PER_STEP
    assert hh % hps == 0

    def page_map(b, hp, pt, ix, *, dh, j, r):
        h = hp * hps + dh
        return (0, pt[b, ix[b, j * hh + h] * ppb + r], h, 0, 0)

    page_specs = [pl.BlockSpec((1, 1, 1, ps, d), functools.partial(page_map, dh=dh, j=j, r=r))
                  for dh in range(hps) for j in range(n_sel) for r in range(ppb)]
    tok = pl.BlockSpec((1, hps, 1, d), lambda b, hp, pt, ix: (b, hp, 0, 0))
    gs = pltpu.PrefetchScalarGridSpec(
        num_scalar_prefetch=2, grid=(nbat, hh // hps),
        in_specs=[tok, tok, tok] + page_specs + page_specs,
        out_specs=tok)
    n_pg = len(page_specs)
    return pl.pallas_call(
        _moba_sample_attn_kernel, grid_spec=gs,
        out_shape=jax.ShapeDtypeStruct((nbat, hh, 1, d), F32),
        compiler_params=_params("parallel", "arbitrary"),
        name="moba_sample_attn",
    )(page_table, idx, q, k_new, v_new, *([pool_k] * n_pg), *([pool_v] * n_pg))


def _cross_sample_kernel(q_ref, mk_ref, mv_ref, o_ref):
    q = q_ref[0]
    outs = []
    for hd in range(CROSS_HEADS):
        sl = slice(hd * HEAD_DIM, (hd + 1) * HEAD_DIM)
        q16 = jnp.broadcast_to(q[:, sl], (8, HEAD_DIM)).astype(BF16)
        s = _dot_nt(q16, mk_ref[0, :, hd, :].astype(BF16)) * HEAD_DIM ** -0.5
        e = jnp.exp(s - jnp.max(s, axis=-1, keepdims=True))
        p = e / jnp.sum(e, axis=-1, keepdims=True)
        outs.append(_dot(p.astype(BF16), mv_ref[0, :, hd, :].astype(BF16))[0:1, :])
    o_ref[0] = jnp.concatenate(outs, axis=-1)


def _cross_sample(q, mem_k, mem_v):
    nbat = q.shape[0]
    row = lambda a: pl.BlockSpec((1,) + a.shape[1:], lambda b: (b,) + (0,) * (a.ndim - 1))
    return pl.pallas_call(
        _cross_sample_kernel,
        grid=(nbat,),
        in_specs=[row(q), row(mem_k), row(mem_v)],
        out_specs=row(q),
        out_shape=jax.ShapeDtypeStruct(q.shape, F32),
        compiler_params=_params("parallel"),
        name="cross_sample",
    )(q, mem_k, mem_v)


def _rope_tables(pos):
    half = ROPE_DIM // 2
    inv_freq = ROPE_THETA ** (-jnp.arange(0, ROPE_DIM, 2, dtype=F32) / ROPE_DIM)
    ang = pos.astype(F32)[:, None] * inv_freq[None, :]
    cos, sin = jnp.cos(ang), jnp.sin(ang)
    n = pos.shape[0]
    cos_t = jnp.concatenate([cos, cos, jnp.ones((n, HEAD_DIM - ROPE_DIM), F32)], axis=-1)
    sin_t = jnp.concatenate([-sin, sin, jnp.zeros((n, HEAD_DIM - ROPE_DIM), F32)], axis=-1)
    return cos_t, sin_t


def kernel(x_prompt, x_sample, cache_moba_k, cache_moba_v, page_table, state_gdn, state_conv, cache_mem_k, cache_mem_v, mem_prompt, w_ffn1_norm, w_ffn1_gu, w_ffn1_down, w_mix_norm, w_in, gdn_conv_w, gdn_a_log, gdn_dt_bias, gdn_out_norm, w_out, w_cross_norm, w_cross_q, w_cross_out, w_mem_norm, w_mem_kv, w_ffn2_norm, w_ffn2_gu, w_ffn2_down, w_final_norm):
    assert x_prompt.shape[0] == 1 and x_sample.shape[1] == 1 and w_in.shape[0] == 1
    s_len, d_model = x_prompt.shape[1], x_prompt.shape[2]
    nbat = x_sample.shape[0]
    past_len = page_table.shape[1] * PAGE_SIZE
    assert past_len % MOBA_BLOCK == 0

    row = lambda v: v.reshape(1, -1).astype(F32)
    g_ffn1, g_mix, g_cross, g_mem, g_ffn2 = (row(w[0]) for w in (w_ffn1_norm, w_mix_norm, w_cross_norm, w_mem_norm, w_ffn2_norm))
    g_final = row(w_final_norm)
    gu1, dn1 = w_ffn1_gu[0].astype(BF16), w_ffn1_down[0].astype(BF16)
    gu2, dn2 = w_ffn2_gu[0].astype(BF16), w_ffn2_down[0].astype(BF16)
    gdn_end = 4 * GDN_WIDTH
    ba_end = gdn_end + 2 * GDN_HEADS
    w_main = jnp.concatenate([w_in[0][:, :gdn_end], w_in[0][:, ba_end:]], axis=1).astype(BF16)
    w_ba = jnp.pad(w_in[0][:, gdn_end:ba_end], ((0, 0), (0, LANES - 2 * GDN_HEADS))).astype(BF16)
    wo_gdn, wo_moba = w_out[0][:GDN_WIDTH].astype(BF16), w_out[0][GDN_WIDTH:].astype(BF16)
    wq_c, wo_c = w_cross_q[0].astype(BF16), w_cross_out[0].astype(BF16)
    w_kv = w_mem_kv[0].astype(BF16)
    conv_w = gdn_conv_w[0]
    lane_pad = lambda v: jnp.pad(v.reshape(1, -1).astype(F32), ((0, 0), (GDN_HEADS, LANES - 2 * GDN_HEADS)))
    alog_row, dtb_row = lane_pad(gdn_a_log[0]), lane_pad(gdn_dt_bias[0])
    onorm = row(gdn_out_norm[0])
    moba_col0 = gdn_end

    xp = x_prompt[0]
    tm = _tile(s_len, 512)
    half = nbat // 2
    h, kmean_lo = _ffn(xp, g_ffn1, gu1, dn1, g_final, final_norm=False, tm=tm, tf=512,
                       paged=(page_table, cache_moba_k, 0, half))
    cols, ba = _rms_matmul(h, g_mix, w_main, tm=_tile(s_len, 1024), tn=1024, w_side=w_ba)
    o_gdn, p_gdn_state, conv_tail = _gdn_prompt(cols, ba, conv_w, alog_row, dtb_row, onorm, chunks_per_step=2)
    q16, p_k, k16, p_v, vt16, bias = _moba_prep(cols, moba_col0)
    o_moba = _moba_attn(q16, k16, vt16, bias, hg=4)
    mem_kv = _rms_matmul(mem_prompt[0], g_mem, w_kv, tm=mem_prompt.shape[1], tn=2 * CROSS_WIDTH)
    mem_k, mem_v = mem_kv[:, :CROSS_WIDTH], mem_kv[:, CROSS_WIDTH:]
    h = _cross_prompt(h, o_gdn, o_moba, wo_gdn, wo_moba, g_cross, wq_c, mem_k.astype(BF16), mem_v.astype(BF16), wo_c, tm=tm)
    y_prompt, kmean_hi = _ffn(h, g_ffn2, gu2, dn2, g_final, final_norm=True, tm=tm, tf=512,
                              paged=(page_table, cache_moba_k, half, nbat - half))
    kmean = jnp.concatenate([kmean_lo, kmean_hi], axis=0)

    xs = x_sample[:, 0]
    hs = _ffn(xs, g_ffn1, gu1, dn1, g_final, final_norm=False, tm=nbat, tf=512)
    cols_s, ba_s = _rms_matmul(hs, g_mix, w_main, tm=nbat, tn=1024, w_side=w_ba)
    o_gdn_s, s_gdn_state, s_conv = _gdn_sample(
        cols_s[:, None, :GDN_CONV_CH], state_conv[0], cols_s[:, None, GDN_CONV_CH:gdn_end], ba_s[:, None, :],
        conv_w, alog_row, dtb_row, onorm, state_gdn[0])
    mq, mk, mv = (cols_s[:, moba_col0 + j * MOBA_WIDTH:moba_col0 + (j + 1) * MOBA_WIDTH].reshape(nbat, MOBA_HEADS, HEAD_DIM) for j in range(3))
    cos_s, sin_s = _rope_tables(jnp.full((1,), past_len))
    n_sel = min(MOBA_TOPK, past_len // MOBA_BLOCK)
    q_r, k_r, idx = _moba_sample_gate(mq, mk, cos_s, sin_s, kmean, n_sel)
    o_moba_s = _moba_sample_attn(page_table, idx, q_r[:, :, None, :], k_r[:, :, None, :], mv[:, :, None, :],
                                 cache_moba_k, cache_moba_v)
    hs = _proj_res(hs, [(o_gdn_s[:, 0], wo_gdn), (o_moba_s.reshape(nbat, MOBA_WIDTH), wo_moba)], tm=nbat)
    q_c = _rms_matmul(hs, g_cross, wq_c, tm=nbat, tn=CROSS_WIDTH)
    o_c = _cross_sample(q_c[:, None, :], cache_mem_k[0], cache_mem_v[0])
    hs = _proj_res(hs, [(o_c[:, 0], wo_c)], tm=nbat)
    y_sample = _ffn(hs, g_ffn2, gu2, dn2, g_final, final_norm=True, tm=nbat, tf=512)

    return (y_prompt[None], y_sample[:, None, :],
            p_k[None, None], p_v[None, None],
            p_gdn_state[None, None], conv_tail[8 - (CONV_WIDTH - 1):][None, None],
            mem_k.reshape(1, 1, -1, CROSS_HEADS, HEAD_DIM), mem_v.reshape(1, 1, -1, CROSS_HEADS, HEAD_DIM),
            k_r[None, :, :, None, :], mv[None, :, :, None, :],
            s_gdn_state[None], s_conv[None])
```

```python
import functools

import jax
import jax.numpy as jnp
from jax import lax
from jax.experimental import pallas as pl
from jax.experimental.pallas import tpu as pltpu

F32 = jnp.float32
BF16 = jnp.bfloat16
HIGHEST = lax.Precision.HIGHEST

HEAD_DIM = 128
GDN_HEADS = 8
MOBA_HEADS = 8
GDN_WIDTH = GDN_HEADS * HEAD_DIM
MOBA_WIDTH = MOBA_HEADS * HEAD_DIM
CONV_WIDTH = 4
GDN_CONV_CH = 3 * GDN_WIDTH
GDN_CHUNK = 64
MOBA_BLOCK = 256
MOBA_TOPK = 3
PAGE_SIZE = 128
ROPE_DIM = HEAD_DIM // 4
ROPE_THETA = 500000.0
CROSS_HEADS = 4
CROSS_WIDTH = CROSS_HEADS * HEAD_DIM
NORM_EPS = 1e-6

LANES = 128
MASK_VALUE = -1e30
LOG2_E = 1.4426950408889634
VT_ROWS = HEAD_DIM + 16
VMEM_LIMIT_BYTES = 56 * 1024 * 1024

_NT = (((1,), (1,)), ((), ()))
_TN = (((0,), (0,)), ((), ()))


def _params(*sem):
    return pltpu.CompilerParams(dimension_semantics=sem, vmem_limit_bytes=VMEM_LIMIT_BYTES)


def _rms(x, g):
    return x * lax.rsqrt(jnp.mean(x * x, axis=-1, keepdims=True) + NORM_EPS) * g


def _silu(x):
    return x * jax.nn.sigmoid(x)


def _dot(a, b):
    return jnp.dot(a, b, preferred_element_type=F32)


def _dot_nt(a, b):
    return lax.dot_general(a, b, _NT, preferred_element_type=F32)


def _dot_f32(a, b):
    return jnp.dot(a, b, precision=HIGHEST, preferred_element_type=F32)


def _col(x, c, width):
    return jnp.broadcast_to(x[:, c:c + 1], (x.shape[0], width))


def _tile(m, pref):
    return pref if m % pref == 0 else m


def _rms_matmul_kernel(x_ref, g_ref, w_ref, *refs):
    o_ref, xn_ref = refs[-3] if len(refs) == 4 else refs[0], refs[-1]

    @pl.when(pl.program_id(1) == 0)
    def _():
        xn_ref[...] = _rms(x_ref[...], g_ref[...]).astype(BF16)
        if len(refs) == 4:
            refs[2][...] = _dot(xn_ref[...], refs[0][...])

    o_ref[...] = _dot(xn_ref[...], w_ref[...])


def _rms_matmul(x, g, w, *, tm, tn, w_side=None):
    m, d = x.shape
    n = w.shape[1]
    in_specs = [pl.BlockSpec((tm, d), lambda i, j: (i, 0)),
                pl.BlockSpec((1, d), lambda i, j: (0, 0)),
                pl.BlockSpec((d, tn), lambda i, j: (0, j))]
    out_specs = [pl.BlockSpec((tm, tn), lambda i, j: (i, j))]
    out_shape = [jax.ShapeDtypeStruct((m, n), F32)]
    args = [x, g, w]
    if w_side is not None:
        in_specs.append(pl.BlockSpec(w_side.shape, lambda i, j: (0, 0)))
        out_specs.append(pl.BlockSpec((tm, w_side.shape[1]), lambda i, j: (i, 0)))
        out_shape.append(jax.ShapeDtypeStruct((m, w_side.shape[1]), F32))
        args.append(w_side)
    outs = pl.pallas_call(
        _rms_matmul_kernel,
        grid=(m // tm, n // tn),
        in_specs=in_specs, out_specs=out_specs, out_shape=out_shape,
        scratch_shapes=[pltpu.VMEM((tm, d), BF16)],
        compiler_params=_params("parallel", "arbitrary"),
        name="rms_matmul",
    )(*args)
    return outs if w_side is not None else outs[0]


_PAGES_PER_STEP = 8


def _block_key_means(page_refs, o_ref, blocks=None):
    ppb = MOBA_BLOCK // PAGE_SIZE
    for blk in (range(len(page_refs) // ppb) if blocks is None else blocks):
        tot = jnp.sum(page_refs[blk * ppb][0, 0], axis=1)
        for r in range(1, ppb):
            tot = tot + jnp.sum(page_refs[blk * ppb + r][0, 0], axis=1)
        o_ref[0, blk] = tot * (1.0 / MOBA_BLOCK)


def _ffn_kernel(*refs, final_norm, n_pages):
    if n_pages:
        refs = refs[1:]
    x_ref, g_ref, wg_ref, wu_ref, wd_ref, fg_ref = refs[:6]
    page_refs = refs[6:6 + n_pages]
    o_ref, xn_ref = refs[6 + n_pages], refs[-1]
    j = pl.program_id(1)

    @pl.when(j == 0)
    def _():
        xn_ref[...] = _rms(x_ref[...], g_ref[...]).astype(BF16)
        o_ref[...] = jnp.zeros_like(o_ref)

    xn = xn_ref[...]
    tf = wg_ref.shape[1]
    parts = 2 if n_pages else 1
    acc = None
    for s in range(parts):
        cs = slice(s * tf // parts, (s + 1) * tf // parts)
        act = _silu(_dot(xn, wg_ref[:, cs])) * _dot(xn, wu_ref[:, cs])
        part = _dot(act.astype(BF16), wd_ref[cs, :])
        acc = part if acc is None else acc + part
        if n_pages:
            nblk = n_pages // (MOBA_BLOCK // PAGE_SIZE)
            _block_key_means(page_refs, refs[7 + n_pages], range(s * nblk // parts, (s + 1) * nblk // parts))
    o_ref[...] += acc

    @pl.when(j == pl.num_programs(1) - 1)
    def _():
        h = x_ref[...] + 0.5 * o_ref[...]
        if final_norm:
            h = _rms(h, fg_ref[...])
        o_ref[...] = h


def _kmean_slots(page_table_rows, n_pages):
    return page_table_rows * (n_pages // _PAGES_PER_STEP)


def _ffn(x, g, w_gu, w_down, fg, *, final_norm, tm, tf, paged=None):
    m, d = x.shape
    f = w_down.shape[0]
    nf = f // tf
    grid = (m // tm, nf)
    in_specs = [pl.BlockSpec((tm, d), lambda i, j, *_: (i, 0)),
                pl.BlockSpec((1, d), lambda i, j, *_: (0, 0)),
                pl.BlockSpec((d, tf), lambda i, j, *_: (0, j)),
                pl.BlockSpec((d, tf), lambda i, j, *_: (0, nf + j)),
                pl.BlockSpec((tf, d), lambda i, j, *_: (j, 0)),
                pl.BlockSpec((1, d), lambda i, j, *_: (0, 0))]
    out_spec = pl.BlockSpec((tm, d), lambda i, j, *_: (i, 0))
    out_shape = jax.ShapeDtypeStruct((m, d), F32)
    scratch = [pltpu.VMEM((tm, d), BF16)]
    if paged is None:
        return pl.pallas_call(
            functools.partial(_ffn_kernel, final_norm=final_norm, n_pages=0),
            grid=grid, in_specs=in_specs, out_specs=out_spec, out_shape=out_shape, scratch_shapes=scratch,
            compiler_params=_params("parallel", "arbitrary"), name="ffn",
        )(x, g, w_gu, w_gu, w_down, fg)

    page_table, pool_k, row0, n_rows = paged
    n_pages = page_table.shape[1]
    _, _, hh, ps, hd = pool_k.shape
    groups = n_pages // _PAGES_PER_STEP
    n_slots = _kmean_slots(n_rows, n_pages)
    assert n_pages % _PAGES_PER_STEP == 0 and n_slots <= grid[0] * grid[1]
    ppb = MOBA_BLOCK // PAGE_SIZE

    def slot(i, j):
        t = jnp.minimum(i * nf + j, n_slots - 1)
        return lax.div(t, groups), lax.rem(t, groups)

    def page_map(i, j, pt, *, r):
        b, c = slot(i, j)
        return (0, pt[row0 + b, c * _PAGES_PER_STEP + r], 0, 0, 0)

    def kmean_map(i, j, pt):
        b, c = slot(i, j)
        return (b, c, 0, 0)

    page_specs = [pl.BlockSpec((1, 1, hh, ps, hd), functools.partial(page_map, r=r)) for r in range(_PAGES_PER_STEP)]
    gs = pltpu.PrefetchScalarGridSpec(
        num_scalar_prefetch=1, grid=grid, in_specs=in_specs + page_specs,
        out_specs=[out_spec, pl.BlockSpec((1, _PAGES_PER_STEP // ppb, hh, hd), kmean_map)],
        scratch_shapes=scratch)
    return pl.pallas_call(
        functools.partial(_ffn_kernel, final_norm=final_norm, n_pages=_PAGES_PER_STEP),
        grid_spec=gs,
        out_shape=[out_shape, jax.ShapeDtypeStruct((n_rows, n_pages // ppb, hh, hd), F32)],
        compiler_params=_params("arbitrary", "arbitrary"), name="ffn_kmean",
    )(page_table, x, g, w_gu, w_gu, w_down, fg, *([pool_k] * _PAGES_PER_STEP))


def _proj_res_kernel(*refs):
    res_ref, o_ref = refs[0], refs[-1]
    pairs = refs[1:-1]
    acc = res_ref[...]
    for a_ref, w_ref in zip(pairs[0::2], pairs[1::2]):
        acc = acc + _dot(a_ref[...].astype(BF16), w_ref[...])
    o_ref[...] = acc


def _proj_res(res, pairs, *, tm):
    m, d = res.shape
    in_specs = [pl.BlockSpec((tm, d), lambda i: (i, 0))]
    args = [res]
    for a, w in pairs:
        in_specs.append(pl.BlockSpec((tm, a.shape[1]), lambda i: (i, 0)))
        in_specs.append(pl.BlockSpec(w.shape, lambda i: (0, 0)))
        args += [a, w]
    return pl.pallas_call(
        _proj_res_kernel,
        grid=(m // tm,),
        in_specs=in_specs,
        out_specs=pl.BlockSpec((tm, d), lambda i: (i, 0)),
        out_shape=jax.ShapeDtypeStruct((m, d), F32),
        compiler_params=_params("parallel"),
        name="proj_res",
    )(*args)


def _cross_kernel(res_ref, a_ref, b_ref, wa_ref, wb_ref, g_ref, wq_ref, mk_ref, mv_ref, wo_ref, o_ref):
    h = res_ref[...] + _dot(a_ref[...].astype(BF16), wa_ref[...]) + _dot(b_ref[...].astype(BF16), wb_ref[...])
    q = _dot(_rms(h, g_ref[...]).astype(BF16), wq_ref[...])
    outs = []
    for hd in range(CROSS_HEADS):
        sl = slice(hd * HEAD_DIM, (hd + 1) * HEAD_DIM)
        s = _dot_nt(q[:, sl].astype(BF16), mk_ref[:, sl]) * HEAD_DIM ** -0.5
        e = jnp.exp(s - jnp.max(s, axis=-1, keepdims=True))
        p = e / jnp.sum(e, axis=-1, keepdims=True)
        outs.append(_dot(p.astype(BF16), mv_ref[:, sl]).astype(BF16))
    o_ref[...] = h + _dot(jnp.concatenate(outs, axis=-1), wo_ref[...])


def _cross_prompt(res, a, b, wa, wb, g, wq, mem_k, mem_v, wo, *, tm):
    m, d = res.shape
    rows = lambda x: pl.BlockSpec((tm, x.shape[1]), lambda i: (i, 0))
    full = lambda x: pl.BlockSpec(x.shape, lambda i: (0, 0), pipeline_mode=pl.Buffered(1))
    return pl.pallas_call(
        _cross_kernel,
        grid=(m // tm,),
        in_specs=[rows(res), rows(a), rows(b), full(wa), full(wb), full(g), full(wq), full(mem_k), full(mem_v), full(wo)],
        out_specs=pl.BlockSpec((tm, d), lambda i: (i, 0)),
        out_shape=jax.ShapeDtypeStruct((m, d), F32),
        compiler_params=_params("parallel"),
        name="cross_prompt",
    )(res, a, b, wa, wb, g, wq, mem_k, mem_v, wo)


def _split(x):
    hi = x.astype(BF16)
    return hi, (x - hi.astype(F32)).astype(BF16)


def _dot_split(a, b):
    return _dot(a[0], b[0]) + _dot(a[0], b[1]) + _dot(a[1], b[0])


def _gdn_prompt_kernel(qkv_ref, z_ref, ba_ref, cw_ref, alog_ref, dtb_ref, onorm_ref,
                       o_ref, sfin_ref, convn_ref, tail_ref, y_ref, s_ref):
    c = GDN_CHUNK
    rows = qkv_ref.shape[0]
    nck = rows // c
    i = pl.program_id(0)

    @pl.when(i == 0)
    def _():
        tail_ref[...] = jnp.zeros_like(tail_ref)
        s_ref[...] = jnp.zeros_like(s_ref)

    x = qkv_ref[...]
    tail = tail_ref[...]
    row8 = lax.broadcasted_iota(jnp.int32, tail.shape, 0)
    y = cw_ref[CONV_WIDTH - 1:CONV_WIDTH, :] * x
    for k in range(1, CONV_WIDTH):
        rolled = pltpu.roll(x, k, 0)
        head = jnp.where(row8 < k, pltpu.roll(tail, k, 0), rolled[0:8])
        y = y + cw_ref[CONV_WIDTH - 1 - k:CONV_WIDTH - k, :] * jnp.concatenate([head, rolled[8:]], axis=0)
    y_ref[...] = _silu(y)
    tail_ref[...] = x[rows - 8:]
    convn_ref[...] = x[rows - 8:]

    ba = ba_ref[...]
    beta = jax.nn.sigmoid(ba)
    g = -jnp.exp(alog_ref[...]) * jax.nn.softplus(ba + dtb_ref[...])
    row = lax.broadcasted_iota(jnp.int32, (c, c), 0)
    colv = lax.broadcasted_iota(jnp.int32, (c, c), 1)
    causal = row >= colv
    strict = row > colv
    eye = (row == colv).astype(F32)
    tri = causal.astype(F32)
    heads = range(GDN_HEADS)

    work = []
    for ck in range(nck):
        r0 = ck * c
        gc = _dot_f32(tri, g[r0:r0 + c])
        gc_t = gc.T
        eg = jnp.exp(gc)
        g_last = gc[c - 1:c, :]
        eg_last = jnp.exp(g_last)
        k_decay = jnp.exp(g_last - gc)
        for h in heads:
            gh = GDN_HEADS + h
            qh = y_ref[r0:r0 + c, h * HEAD_DIM:(h + 1) * HEAD_DIM]
            kh = y_ref[r0:r0 + c, GDN_WIDTH + h * HEAD_DIM:GDN_WIDTH + (h + 1) * HEAD_DIM]
            vh = y_ref[r0:r0 + c, 2 * GDN_WIDTH + h * HEAD_DIM:2 * GDN_WIDTH + (h + 1) * HEAD_DIM]
            qn = qh * lax.rsqrt(jnp.sum(qh * qh, axis=-1, keepdims=True) + NORM_EPS) * HEAD_DIM ** -0.5
            kn = kh * lax.rsqrt(jnp.sum(kh * kh, axis=-1, keepdims=True) + NORM_EPS)
            b_col = _col(beta[r0:r0 + c], h, HEAD_DIM)
            eg_col = _col(eg, gh, HEAD_DIM)
            diff = _col(gc, gh, c) - jnp.broadcast_to(gc_t[gh:gh + 1, :], (c, c))
            decay = jnp.exp(jnp.where(causal, diff, -jnp.inf))
            kb = kn * b_col
            kn16 = kn.astype(BF16)
            qn16 = qn.astype(BF16)
            work.append(dict(
                a=_dot_nt(kb.astype(BF16), kn16) * jnp.where(strict, decay, 0.0),
                attn=(_dot_nt(qn16, kn16) * decay).astype(BF16),
                vb=(vh * b_col).astype(BF16),
                kbg=(kb * eg_col).astype(BF16),
                qg=(qn * eg_col).astype(BF16),
                kd=(kn * _col(k_decay, gh, HEAD_DIM)).astype(BF16),
                eg_last=eg_last[:, gh:gh + 1]))

    ps = [_split(-wk["a"]) for wk in work]
    ts = [eye - wk["a"] for wk in work]
    for _ in range(5):
        ps = [_split(_dot_split(p, p)) for p in ps]
        ts = [t + _dot_split(_split(t), p) for t, p in zip(ts, ps)]
    for wk, t in zip(work, ts):
        t16 = t.astype(BF16)
        wk["u"] = _dot(t16, wk["vb"])
        wk["w"] = _dot(t16, wk["kbg"]).astype(BF16)

    for ck in range(nck):
        r0 = ck * c
        wks = work[ck * GDN_HEADS:(ck + 1) * GDN_HEADS]
        states = [s_ref[h] for h in heads]
        s16 = [s.astype(BF16) for s in states]
        v_new = [(wk["u"] - _dot(wk["w"], s16[h])).astype(BF16) for h, wk in zip(heads, wks)]
        outs = [_dot(wk["qg"], s16[h]) + _dot(wk["attn"], v_new[h]) for h, wk in zip(heads, wks)]
        for h, wk in zip(heads, wks):
            s_ref[h] = states[h] * wk["eg_last"] + lax.dot_general(wk["kd"], v_new[h], _TN, preferred_element_type=F32)
        for h, o in zip(heads, outs):
            sl = slice(h * HEAD_DIM, (h + 1) * HEAD_DIM)
            o = o * lax.rsqrt(jnp.mean(o * o, axis=-1, keepdims=True) + NORM_EPS) * onorm_ref[...]
            o_ref[r0:r0 + c, sl] = o * _silu(z_ref[r0:r0 + c, sl])

    @pl.when(i == pl.num_programs(0) - 1)
    def _():
        sfin_ref[...] = s_ref[...]


def _gdn_prompt(cols, ba, conv_w, alog_row, dtb_row, onorm, *, chunks_per_step):
    s_len = cols.shape[0]
    c = GDN_CHUNK * chunks_per_step
    assert s_len % c == 0
    full = lambda a: pl.BlockSpec(a.shape, lambda i: (0,) * a.ndim)
    return pl.pallas_call(
        _gdn_prompt_kernel,
        grid=(s_len // c,),
        in_specs=[pl.BlockSpec((c, GDN_CONV_CH), lambda i: (i, 0)),
                  pl.BlockSpec((c, GDN_WIDTH), lambda i: (i, GDN_CONV_CH // GDN_WIDTH)),
                  pl.BlockSpec((c, LANES), lambda i: (i, 0)),
                  full(conv_w), full(alog_row), full(dtb_row), full(onorm)],
        out_specs=[pl.BlockSpec((c, GDN_WIDTH), lambda i: (i, 0)),
                   pl.BlockSpec((GDN_HEADS, HEAD_DIM, HEAD_DIM), lambda i: (0, 0, 0)),
                   pl.BlockSpec((8, GDN_CONV_CH), lambda i: (0, 0))],
        out_shape=[jax.ShapeDtypeStruct((s_len, GDN_WIDTH), F32),
                   jax.ShapeDtypeStruct((GDN_HEADS, HEAD_DIM, HEAD_DIM), F32),
                   jax.ShapeDtypeStruct((8, GDN_CONV_CH), F32)],
        scratch_shapes=[pltpu.VMEM((8, GDN_CONV_CH), F32),
                        pltpu.VMEM((c, GDN_CONV_CH), F32),
                        pltpu.VMEM((GDN_HEADS, HEAD_DIM, HEAD_DIM), F32)],
        compiler_params=_params("arbitrary"),
        name="gdn_prompt",
    )(cols, cols, ba, conv_w, alog_row, dtb_row, onorm)


def _gdn_sample_kernel(x_ref, cs_ref, z_ref, ba_ref, cw_ref, alog_ref, dtb_ref, onorm_ref, s0_ref,
                       o_ref, s_ref, convn_ref):
    x = x_ref[0]
    cs = cs_ref[0]
    y = cw_ref[CONV_WIDTH - 1:CONV_WIDTH, :] * x
    for j in range(CONV_WIDTH - 1):
        y = y + cw_ref[j:j + 1, :] * cs[j:j + 1, :]
    y = _silu(y)
    convn_ref[0] = jnp.concatenate([cs[1:CONV_WIDTH - 1, :], x], axis=0)

    ba = ba_ref[0]
    beta = jax.nn.sigmoid(ba)
    eg = jnp.exp(-jnp.exp(alog_ref[...]) * jax.nn.softplus(ba + dtb_ref[...]))
    z = z_ref[0]

    def head_rows(base):
        return jnp.concatenate([y[:, base + h * HEAD_DIM:base + (h + 1) * HEAD_DIM] for h in range(GDN_HEADS)], axis=0)

    q8 = head_rows(0)
    k8 = head_rows(GDN_WIDTH)
    v8 = head_rows(2 * GDN_WIDTH)
    q8 = q8 * lax.rsqrt(jnp.sum(q8 * q8, axis=-1, keepdims=True) + NORM_EPS) * HEAD_DIM ** -0.5
    k8 = k8 * lax.rsqrt(jnp.sum(k8 * k8, axis=-1, keepdims=True) + NORM_EPS)
    qk8 = jnp.sum(q8 * k8, axis=-1, keepdims=True)
    q_t = q8.T
    k_t = k8.T
    outs = []
    for h in range(GDN_HEADS):
        s = s0_ref[0, h]
        k_col = _col(k_t, h, HEAD_DIM)
        q_col = _col(q_t, h, HEAD_DIM)
        e = eg[:, GDN_HEADS + h:GDN_HEADS + h + 1]
        k_s = jnp.sum(k_col * s, axis=0, keepdims=True)
        q_s = jnp.sum(q_col * s, axis=0, keepdims=True)
        v_new = beta[:, h:h + 1] * (v8[h:h + 1, :] - e * k_s)
        o = e * q_s + qk8[h:h + 1, :] * v_new
        s_ref[0, h] = s * e + k_col * v_new
        o = o * lax.rsqrt(jnp.mean(o * o, axis=-1, keepdims=True) + NORM_EPS) * onorm_ref[...]
        outs.append(o * _silu(z[:, h * HEAD_DIM:(h + 1) * HEAD_DIM]))
    o_ref[0] = jnp.concatenate(outs, axis=-1)


def _gdn_sample(x, conv_state, z, ba, conv_w, alog_row, dtb_row, onorm, s0):
    nb = x.shape[0]
    full = lambda a: pl.BlockSpec(a.shape, lambda b: (0,) * a.ndim)
    row = lambda a: pl.BlockSpec((1,) + a.shape[1:], lambda b: (b,) + (0,) * (a.ndim - 1))
    return pl.pallas_call(
        _gdn_sample_kernel,
        grid=(nb,),
        in_specs=[row(x), row(conv_state), row(z), row(ba), full(conv_w), full(alog_row), full(dtb_row),
                  full(onorm), row(s0)],
        out_specs=[pl.BlockSpec((1, 1, GDN_WIDTH), lambda b: (b, 0, 0)), row(s0), row(conv_state)],
        out_shape=[jax.ShapeDtypeStruct((nb, 1, GDN_WIDTH), F32),
                   jax.ShapeDtypeStruct(s0.shape, F32),
                   jax.ShapeDtypeStruct(conv_state.shape, F32)],
        compiler_params=_params("parallel"),
        name="gdn_sample",
    )(x, conv_state, z, ba, conv_w, alog_row, dtb_row, onorm, s0)


def _rope(x, cos, sin_signed):
    lane = lax.broadcasted_iota(jnp.int32, x.shape, x.ndim - 1)
    half = ROPE_DIM // 2
    partner = jnp.where(lane < half, pltpu.roll(x, LANES - half, x.ndim - 1), pltpu.roll(x, half, x.ndim - 1))
    return x * cos + partner * sin_signed


def _moba_prep_kernel(q_ref, k_ref, v_ref, cos_in_ref, sin_in_ref, cos_blk_ref, sin_blk_ref,
                      q16_ref, kf_ref, k16_ref, vf_ref, vt16_ref, bias_ref, kmean_ref, *, n_sel):
    i = pl.program_id(0)
    nb = kmean_ref.shape[1]
    t = q_ref.shape[0]

    @pl.when(i == 0)
    def _():
        kmean_ref[...] = jnp.zeros_like(kmean_ref)

    cb, sb = cos_blk_ref[pl.ds(i, 1), :], sin_blk_ref[pl.ds(i, 1), :]
    cos = cb * cos_in_ref[...] - sb * sin_in_ref[...]
    sin = sb * cos_in_ref[...] + cb * sin_in_ref[...]
    blk = lax.broadcasted_iota(jnp.int32, (nb, t), 0)
    past = blk < i
    for h in range(MOBA_HEADS):
        sl = slice(h * HEAD_DIM, (h + 1) * HEAD_DIM)
        q = _rope(q_ref[:, sl], cos, sin)
        k = _rope(k_ref[:, sl], cos, sin)
        v = v_ref[:, sl]
        q16_ref[h] = (q * (HEAD_DIM ** -0.5 * LOG2_E)).astype(BF16)
        kf_ref[h] = k
        k16_ref[h] = k.astype(BF16)
        vf_ref[h] = v
        vt16_ref[h, 0, 0:HEAD_DIM, :] = v.T.astype(BF16)
        vt16_ref[h, 0, HEAD_DIM:VT_ROWS, :] = (lax.broadcasted_iota(jnp.int32, (VT_ROWS - HEAD_DIM, t), 0) == 0).astype(BF16)
        km_hi, km_lo = _split(kmean_ref[h])
        q_hi, q_lo = _split(q)
        gate = _dot_nt(km_hi, q_hi) + _dot_nt(km_hi, q_lo) + _dot_nt(km_lo, q_hi)
        gate = jnp.where(past, gate, -jnp.inf)
        chosen = jnp.zeros((nb, t), F32)
        for _ in range(n_sel):
            top = jnp.max(gate, axis=0, keepdims=True)
            first = jnp.min(jnp.where(gate == top, blk, nb), axis=0, keepdims=True)
            pick = (blk == first) & (top > -jnp.inf)
            chosen = jnp.where(pick, 1.0, chosen)
            gate = jnp.where(pick, -jnp.inf, gate)
        bias_ref[h] = jnp.where(chosen > 0.0, 0.0, MASK_VALUE)
        kmean_ref[h, pl.ds(i, 1), :] = jnp.mean(k, axis=0, keepdims=True)


def _moba_prep(cols, col0):
    s_len = cols.shape[0]
    t = MOBA_BLOCK
    assert s_len % t == 0
    nb = s_len // t
    hh = MOBA_HEADS
    cb = col0 // MOBA_WIDTH
    n_sel = min(MOBA_TOPK, nb - 1)
    cos_in, sin_in = _rope_tables(jnp.arange(t))
    cos_blk, sin_blk = _rope_tables(jnp.arange(nb) * t)
    full = lambda a: pl.BlockSpec(a.shape, lambda i: (0, 0))
    return pl.pallas_call(
        functools.partial(_moba_prep_kernel, n_sel=n_sel),
        grid=(nb,),
        in_specs=[pl.BlockSpec((t, MOBA_WIDTH), lambda i: (i, cb)),
                  pl.BlockSpec((t, MOBA_WIDTH), lambda i: (i, cb + 1)),
                  pl.BlockSpec((t, MOBA_WIDTH), lambda i: (i, cb + 2)),
                  full(cos_in), full(sin_in), full(cos_blk), full(sin_blk)],
        out_specs=[pl.BlockSpec((hh, t, HEAD_DIM), lambda i: (0, i, 0)),
                   pl.BlockSpec((hh, t, HEAD_DIM), lambda i: (0, i, 0)),
                   pl.BlockSpec((hh, t, HEAD_DIM), lambda i: (0, i, 0)),
                   pl.BlockSpec((hh, t, HEAD_DIM), lambda i: (0, i, 0)),
                   pl.BlockSpec((hh, 1, VT_ROWS, t), lambda i: (0, i, 0, 0)),
                   pl.BlockSpec((hh, nb, t), lambda i: (0, 0, i))],
        out_shape=[jax.ShapeDtypeStruct((hh, s_len, HEAD_DIM), BF16),
                   jax.ShapeDtypeStruct((hh, s_len, HEAD_DIM), F32),
                   jax.ShapeDtypeStruct((hh, s_len, HEAD_DIM), BF16),
                   jax.ShapeDtypeStruct((hh, s_len, HEAD_DIM), F32),
                   jax.ShapeDtypeStruct((hh, nb, VT_ROWS, t), BF16),
                   jax.ShapeDtypeStruct((hh, nb, s_len), F32)],
        scratch_shapes=[pltpu.VMEM((hh, nb, HEAD_DIM), F32)],
        compiler_params=_params("arbitrary"),
        name="moba_prep",
    )(cols, cols, cols, cos_in, sin_in, cos_blk, sin_blk)


_KV_PAIR = 2


def _moba_attn_kernel(q_ref, k_ref, vt_ref, bias_ref, o_ref, acc_ref, s_ref, p_ref):
    hg, tq, _ = q_ref.shape
    tk = vt_ref.shape[3]
    own = pl.program_id(1)
    diag = own // _KV_PAIR
    kpos = lax.broadcasted_iota(jnp.int32, (tk, tq), 0)
    qpos = lax.broadcasted_iota(jnp.int32, (tk, tq), 1)
    causal_bias = jnp.where(kpos <= qpos, 0.0, MASK_VALUE)

    def scores(j, slot, diagonal=False):
        g = diag if diagonal else jnp.minimum(j - 1, diag)
        real = j <= diag
        off = pl.multiple_of(g * (_KV_PAIR * tk), _KV_PAIR * tk)
        for h in range(hg):
            s = _dot_nt(k_ref[h, pl.ds(off, _KV_PAIR * tk), :], q_ref[h])
            for r in range(_KV_PAIR):
                kb = g * _KV_PAIR + r
                if diagonal:
                    row = jnp.broadcast_to(bias_ref[h, pl.ds(kb, 1), :], (tk, tq))
                    bias = jnp.where(kb == own, causal_bias, row)
                else:
                    bias = jnp.where(real, bias_ref[h, pl.ds(kb, 1), :], MASK_VALUE)
                s_ref[slot, h, r * tk:(r + 1) * tk, :] = s[r * tk:(r + 1) * tk] + bias

    def softmax(slot, ms):
        new, alphas = [], []
        for h in range(hg):
            s = s_ref[slot, h]
            m_new = jnp.maximum(ms[h], jnp.max(s, axis=0, keepdims=True))
            p_ref[slot, h] = jnp.exp2(s - m_new).astype(BF16)
            new.append(m_new)
            alphas.append(jnp.exp2(ms[h] - m_new))
        return tuple(new), tuple(alphas)

    def accumulate(j, slot, alphas):
        g = jnp.where(j == 0, diag, jnp.minimum(j - 1, diag))
        for h in range(hg):
            pv = _dot(vt_ref[h, g * _KV_PAIR], p_ref[slot, h, 0:tk, :])
            for r in range(1, _KV_PAIR):
                pv = pv + _dot(vt_ref[h, g * _KV_PAIR + r], p_ref[slot, h, r * tk:(r + 1) * tk, :])
            acc_ref[h] = acc_ref[h] * alphas[h] + pv

    acc_ref[...] = jnp.zeros_like(acc_ref)
    m0 = jnp.full((1, tq), MASK_VALUE, F32)
    scores(0, 0, diagonal=True)
    scores(1, 1)
    carry = softmax(0, (m0,) * hg)

    def step(j, cur, carry):
        ms, alphas = carry
        scores(j + 1, 1 - cur)
        new = softmax(cur, ms)
        accumulate(j - 1, 1 - cur, alphas)
        return new

    def two_steps(t, carry):
        return step(2 * t + 2, 0, step(2 * t + 1, 1, carry))

    half = (diag + 1) // 2
    _, alphas = lax.fori_loop(0, half, two_steps, carry)
    accumulate(2 * half, 0, alphas)
    for h in range(hg):
        acc = acc_ref[h]
        o_ref[:, h * HEAD_DIM:(h + 1) * HEAD_DIM] = (acc[0:HEAD_DIM] / acc[HEAD_DIM:HEAD_DIM + 1]).T


def _moba_attn(q16, k16, vt16, bias, *, hg):
    hh, s_len, d = q16.shape
    nb, rows, tk = vt16.shape[1:]
    assert nb % _KV_PAIR == 0 and hh % hg == 0
    once = pl.Buffered(1)
    return pl.pallas_call(
        _moba_attn_kernel,
        grid=(hh // hg, nb),
        in_specs=[pl.BlockSpec((hg, tk, d), lambda h, i: (h, i, 0)),
                  pl.BlockSpec((hg, s_len, d), lambda h, i: (h, 0, 0), pipeline_mode=once),
                  pl.BlockSpec((hg, nb, rows, tk), lambda h, i: (h, 0, 0, 0), pipeline_mode=once),
                  pl.BlockSpec((hg, nb, tk), lambda h, i: (h, 0, i))],
        out_specs=pl.BlockSpec((tk, hg * d), lambda h, i: (i, h)),
        out_shape=jax.ShapeDtypeStruct((s_len, hh * d), F32),
        scratch_shapes=[pltpu.VMEM((hg, rows, tk), F32),
                        pltpu.VMEM((2, hg, _KV_PAIR * tk, tk), F32),
                        pltpu.VMEM((2, hg, _KV_PAIR * tk, tk), BF16)],
        compiler_params=_params("parallel", "arbitrary"),
        name="moba_attn",
    )(q16, k16, vt16, bias)


def _moba_sample_gate_kernel(q_ref, k_ref, cos_ref, sin_ref, km_ref, qr_ref, kr_ref, idx_ref, *, n_sel):
    q = _rope(q_ref[0], cos_ref[...], sin_ref[...])
    kr_ref[0] = _rope(k_ref[0], cos_ref[...], sin_ref[...])
    qr_ref[0] = q
    gate = jnp.sum(km_ref[0] * q, axis=-1)
    nb = gate.shape[0]
    blk = lax.broadcasted_iota(jnp.int32, gate.shape, 0)
    rows = []
    for _ in range(n_sel):
        top = jnp.max(gate, axis=0, keepdims=True)
        first = jnp.min(jnp.where(gate == top, blk, nb), axis=0, keepdims=True)
        rows.append(first)
        gate = jnp.where(blk == first, -jnp.inf, gate)
    idx_ref[0] = jnp.concatenate(rows, axis=0)


def _moba_sample_gate(q, k, cos, sin, kmean, n_sel):
    nbat, hh, d = q.shape
    nb = kmean.shape[1]
    row = lambda a: pl.BlockSpec((1,) + a.shape[1:], lambda b: (b,) + (0,) * (a.ndim - 1))
    full = lambda a: pl.BlockSpec(a.shape, lambda b: (0,) * a.ndim)
    return pl.pallas_call(
        functools.partial(_moba_sample_gate_kernel, n_sel=n_sel),
        grid=(nbat,),
        in_specs=[row(q), row(k), full(cos), full(sin), row(kmean)],
        out_specs=[row(q), row(k), pl.BlockSpec((1, n_sel, hh), lambda b: (b, 0, 0))],
        out_shape=[jax.ShapeDtypeStruct(q.shape, F32), jax.ShapeDtypeStruct(k.shape, F32),
                   jax.ShapeDtypeStruct((nbat, n_sel, hh), jnp.int32)],
        compiler_params=_params("parallel"),
        name="moba_sample_gate",
    )(q, k, cos, sin, kmean)


_SAMPLE_HEADS_PER_STEP = 4


def _moba_sample_attn_kernel(pt_ref, idx_ref, q_ref, kn_ref, vn_ref, *refs):
    del pt_ref, idx_ref
    o_ref = refs[-1]
    hps = q_ref.shape[1]
    n = (len(refs) - 1) // (2 * hps)
    k_refs, v_refs = refs[:n * hps], refs[n * hps:2 * n * hps]
    q8 = [jnp.broadcast_to(q_ref[0, h] * HEAD_DIM ** -0.5, (8, HEAD_DIM)) for h in range(hps)]
    scores = [[_dot_nt(q8[h].astype(BF16), kr[0, 0, 0].astype(BF16)) for kr in k_refs[h * n:(h + 1) * n]] for h in range(hps)]
    for h in range(hps):
        s_new = jnp.sum(q8[h] * kn_ref[0, h], axis=-1, keepdims=True)
        m = s_new
        for s in scores[h]:
            m = jnp.maximum(m, jnp.max(s, axis=-1, keepdims=True))
        e_new = jnp.exp(s_new - m)
        l = e_new
        acc = e_new * vn_ref[0, h]
        for s, vr in zip(scores[h], v_refs[h * n:(h + 1) * n]):
            e = jnp.exp(s - m)
            l = l + jnp.sum(e, axis=-1, keepdims=True)
            acc = acc + _dot(e.astype(BF16), vr[0, 0, 0].astype(BF16))
        o_ref[0, h] = (acc / l)[0:1, :]


def _moba_sample_attn(page_table, idx, q, k_new, v_new, pool_k, pool_v):
    nbat, hh, _, d = q.shape
    n_sel = idx.shape[1]
    idx = idx.reshape(nbat, n_sel * hh)
    ps = pool_k.shape[3]
    ppb = MOBA_BLOCK // PAGE_SIZE
    hps = _SAMPLE_HEADS_PER_STEP
    assert hh % hps == 0

    def page_map(b, hp, pt, ix, *, dh, j, r):
        h = hp * hps + dh
        return (0, pt[b, ix[b, j * hh + h] * ppb + r], h, 0, 0)

    page_specs = [pl.BlockSpec((1, 1, 1, ps, d), functools.partial(page_map, dh=dh, j=j, r=r))
                  for dh in range(hps) for j in range(n_sel) for r in range(ppb)]
    tok = pl.BlockSpec((1, hps, 1, d), lambda b, hp, pt, ix: (b, hp, 0, 0))
    gs = pltpu.PrefetchScalarGridSpec(
        num_scalar_prefetch=2, grid=(nbat, hh // hps),
        in_specs=[tok, tok, tok] + page_specs + page_specs,
        out_specs=tok)
    n_pg = len(page_specs)
    return pl.pallas_call(
        _moba_sample_attn_kernel, grid_spec=gs,
        out_shape=jax.ShapeDtypeStruct((nbat, hh, 1, d), F32),
        compiler_params=_params("parallel", "arbitrary"),
        name="moba_sample_attn",
    )(page_table, idx, q, k_new, v_new, *([pool_k] * n_pg), *([pool_v] * n_pg))


def _cross_sample_kernel(q_ref, mk_ref, mv_ref, o_ref):
    q = q_ref[0]
    outs = []
    for hd in range(CROSS_HEADS):
        sl = slice(hd * HEAD_DIM, (hd + 1) * HEAD_DIM)
        q16 = jnp.broadcast_to(q[:, sl], (8, HEAD_DIM)).astype(BF16)
        s = _dot_nt(q16, mk_ref[0, :, hd, :].astype(BF16)) * HEAD_DIM ** -0.5
        e = jnp.exp(s - jnp.max(s, axis=-1, keepdims=True))
        p = e / jnp.sum(e, axis=-1, keepdims=True)
        outs.append(_dot(p.astype(BF16), mv_ref[0, :, hd, :].astype(BF16))[0:1, :])
    o_ref[0] = jnp.concatenate(outs, axis=-1)


def _cross_sample(q, mem_k, mem_v):
    nbat = q.shape[0]
    row = lambda a: pl.BlockSpec((1,) + a.shape[1:], lambda b: (b,) + (0,) * (a.ndim - 1))
    return pl.pallas_call(
        _cross_sample_kernel,
        grid=(nbat,),
        in_specs=[row(q), row(mem_k), row(mem_v)],
        out_specs=row(q),
        out_shape=jax.ShapeDtypeStruct(q.shape, F32),
        compiler_params=_params("parallel"),
        name="cross_sample",
    )(q, mem_k, mem_v)


def _rope_tables(pos):
    half = ROPE_DIM // 2
    inv_freq = ROPE_THETA ** (-jnp.arange(0, ROPE_DIM, 2, dtype=F32) / ROPE_DIM)
    ang = pos.astype(F32)[:, None] * inv_freq[None, :]
    cos, sin = jnp.cos(ang), jnp.sin(ang)
    n = pos.shape[0]
    cos_t = jnp.concatenate([cos, cos, jnp.ones((n, HEAD_DIM - ROPE_DIM), F32)], axis=-1)
    sin_t = jnp.concatenate([-sin, sin, jnp.zeros((n, HEAD_DIM - ROPE_DIM), F32)], axis=-1)
    return cos_t, sin_t


def kernel(x_prompt, x_sample, cache_moba_k, cache_moba_v, page_table, state_gdn, state_conv, cache_mem_k, cache_mem_v, mem_prompt, w_ffn1_norm, w_ffn1_gu, w_ffn1_down, w_mix_norm, w_in, gdn_conv_w, gdn_a_log, gdn_dt_bias, gdn_out_norm, w_out, w_cross_norm, w_cross_q, w_cross_out, w_mem_norm, w_mem_kv, w_ffn2_norm, w_ffn2_gu, w_ffn2_down, w_final_norm):
    assert x_prompt.shape[0] == 1 and x_sample.shape[1] == 1 and w_in.shape[0] == 1
    s_len, d_model = x_prompt.shape[1], x_prompt.shape[2]
    nbat = x_sample.shape[0]
    past_len = page_table.shape[1] * PAGE_SIZE
    assert past_len % MOBA_BLOCK == 0

    row = lambda v: v.reshape(1, -1).astype(F32)
    g_ffn1, g_mix, g_cross, g_mem, g_ffn2 = (row(w[0]) for w in (w_ffn1_norm, w_mix_norm, w_cross_norm, w_mem_norm, w_ffn2_norm))
    g_final = row(w_final_norm)
    gu1, dn1 = w_ffn1_gu[0].astype(BF16), w_ffn1_down[0].astype(BF16)
    gu2, dn2 = w_ffn2_gu[0].astype(BF16), w_ffn2_down[0].astype(BF16)
    gdn_end = 4 * GDN_WIDTH
    ba_end = gdn_end + 2 * GDN_HEADS
    w_main = jnp.concatenate([w_in[0][:, :gdn_end], w_in[0][:, ba_end:]], axis=1).astype(BF16)
    w_ba = jnp.pad(w_in[0][:, gdn_end:ba_end], ((0, 0), (0, LANES - 2 * GDN_HEADS))).astype(BF16)
    wo_gdn, wo_moba = w_out[0][:GDN_WIDTH].astype(BF16), w_out[0][GDN_WIDTH:].astype(BF16)
    wq_c, wo_c = w_cross_q[0].astype(BF16), w_cross_out[0].astype(BF16)
    w_kv = w_mem_kv[0].astype(BF16)
    conv_w = gdn_conv_w[0]
    lane_pad = lambda v: jnp.pad(v.reshape(1, -1).astype(F32), ((0, 0), (GDN_HEADS, LANES - 2 * GDN_HEADS)))
    alog_row, dtb_row = lane_pad(gdn_a_log[0]), lane_pad(gdn_dt_bias[0])
    onorm = row(gdn_out_norm[0])
    moba_col0 = gdn_end

    xp = x_prompt[0]
    tm = _tile(s_len, 512)
    half = nbat // 2
    h, kmean_lo = _ffn(xp, g_ffn1, gu1, dn1, g_final, final_norm=False, tm=tm, tf=512,
                       paged=(page_table, cache_moba_k, 0, half))
    cols, ba = _rms_matmul(h, g_mix, w_main, tm=_tile(s_len, 1024), tn=1024, w_side=w_ba)
    o_gdn, p_gdn_state, conv_tail = _gdn_prompt(cols, ba, conv_w, alog_row, dtb_row, onorm, chunks_per_step=4)
    q16, p_k, k16, p_v, vt16, bias = _moba_prep(cols, moba_col0)
    o_moba = _moba_attn(q16, k16, vt16, bias, hg=4)
    mem_kv = _rms_matmul(mem_prompt[0], g_mem, w_kv, tm=mem_prompt.shape[1], tn=2 * CROSS_WIDTH)
    mem_k, mem_v = mem_kv[:, :CROSS_WIDTH], mem_kv[:, CROSS_WIDTH:]
    h = _cross_prompt(h, o_gdn, o_moba, wo_gdn, wo_moba, g_cross, wq_c, mem_k.astype(BF16), mem_v.astype(BF16), wo_c, tm=tm)
    y_prompt, kmean_hi = _ffn(h, g_ffn2, gu2, dn2, g_final, final_norm=True, tm=tm, tf=512,
                              paged=(page_table, cache_moba_k, half, nbat - half))
    kmean = jnp.concatenate([kmean_lo, kmean_hi], axis=0)

    xs = x_sample[:, 0]
    hs = _ffn(xs, g_ffn1, gu1, dn1, g_final, final_norm=False, tm=nbat, tf=512)
    cols_s, ba_s = _rms_matmul(hs, g_mix, w_main, tm=nbat, tn=1024, w_side=w_ba)
    o_gdn_s, s_gdn_state, s_conv = _gdn_sample(
        cols_s[:, None, :GDN_CONV_CH], state_conv[0], cols_s[:, None, GDN_CONV_CH:gdn_end], ba_s[:, None, :],
        conv_w, alog_row, dtb_row, onorm, state_gdn[0])
    mq, mk, mv = (cols_s[:, moba_col0 + j * MOBA_WIDTH:moba_col0 + (j + 1) * MOBA_WIDTH].reshape(nbat, MOBA_HEADS, HEAD_DIM) for j in range(3))
    cos_s, sin_s = _rope_tables(jnp.full((1,), past_len))
    n_sel = min(MOBA_TOPK, past_len // MOBA_BLOCK)
    q_r, k_r, idx = _moba_sample_gate(mq, mk, cos_s, sin_s, kmean, n_sel)
    o_moba_s = _moba_sample_attn(page_table, idx, q_r[:, :, None, :], k_r[:, :, None, :], mv[:, :, None, :],
                                 cache_moba_k, cache_moba_v)
    hs = _proj_res(hs, [(o_gdn_s[:, 0], wo_gdn), (o_moba_s.reshape(nbat, MOBA_WIDTH), wo_moba)], tm=nbat)
    q_c = _rms_matmul(hs, g_cross, wq_c, tm=nbat, tn=CROSS_WIDTH)
    o_c = _cross_sample(q_c[:, None, :], cache_mem_k[0], cache_mem_v[0])
    hs = _proj_res(hs, [(o_c[:, 0], wo_c)], tm=nbat)
    y_sample = _ffn(hs, g_ffn2, gu2, dn2, g_final, final_norm=True, tm=nbat, tf=512)

    return (y_prompt[None], y_sample[:, None, :],
            p_k[None, None], p_v[None, None],
            p_gdn_state[None, None], conv_tail[8 - (CONV_WIDTH - 1):][None, None],
            mem_k.reshape(1, 1, -1, CROSS_HEADS, HEAD_DIM), mem_v.reshape(1, 1, -1, CROSS_HEADS, HEAD_DIM),
            k_r[None, :, :, None, :], mv[None, :, :, None, :],
            s_gdn_state[None], s_conv[None])
```
